```python
import math
import jax, jax.numpy as jnp
from jax import lax
import numpy as np

D_MODEL = 1024
BATCH = 16
SEQ = 2048
DEPTH = 2

GRID_W = 64
CTX_LEN = 256

DN_HEADS = 4
DN_DK = 128
DN_DV = 128
DN_CONV = 3
DN_QKV = DN_HEADS * (2 * DN_DK + DN_DV)
RET_HEADS = 4
RET_DK = 128
RET_DV = 128
CHUNK = 64
EVEN_SIZES = (DN_QKV, DN_HEADS * DN_DV, 2 * DN_HEADS, 2 * DN_HEADS,
              RET_HEADS * RET_DK, RET_HEADS * RET_DK, RET_HEADS * RET_DV, RET_HEADS * RET_DV)
EVEN_IN = sum(EVEN_SIZES)
EVEN_MIX = DN_HEADS * DN_DV + RET_HEADS * RET_DV
DIFF_HEADS = 8
DIFF_HD = 64
DIFF_DV = 2 * DIFF_HD
ODD_IN = DIFF_HEADS * (4 * DIFF_HD + DIFF_DV)
ODD_MIX = DIFF_HEADS * DIFF_DV
Q_BLOCK = 128
D_FF = 2816
FFN_CONV = 3

ROPE_BASE = 10000.0
LN_EPS = 1e-5
NORM_EPS = 1e-6
DEEP_ALPHA = (2 * DEPTH) ** 0.25
DEEP_BETA = (8 * DEPTH) ** -0.25
N_EVEN = (DEPTH + 1) // 2
N_ODD = DEPTH // 2

kernel_name = 'hybrid_deltanet_retention_diffattn_dit'


def layer_norm(x, g, b):
    xf = x.astype(jnp.float32)
    mu = jnp.mean(xf, -1, keepdims=True)
    var = jnp.mean(jnp.square(xf - mu), -1, keepdims=True)
    y = (xf - mu) * lax.rsqrt(var + LN_EPS) * g.astype(jnp.float32) + b.astype(jnp.float32)
    return y.astype(x.dtype)


def rms_norm(x):
    xf = x.astype(jnp.float32)
    return xf * lax.rsqrt(jnp.mean(jnp.square(xf), -1, keepdims=True) + NORM_EPS)


def group_norm(x):
    xf = x.astype(jnp.float32)
    mu = jnp.mean(xf, -1, keepdims=True)
    var = jnp.mean(jnp.square(xf - mu), -1, keepdims=True)
    return (xf - mu) * lax.rsqrt(var + NORM_EPS)


def l2_normalize(x):
    xf = x.astype(jnp.float32)
    return xf * lax.rsqrt(jnp.sum(jnp.square(xf), -1, keepdims=True) + NORM_EPS)


def modulate(x, shift, scale):
    return x * (1 + scale) + shift


def to_heads(x, h):
    b, l, _ = x.shape
    return x.reshape(b, l, h, -1).transpose(0, 2, 1, 3)


def merge_heads(x):
    b, h, l, d = x.shape
    return x.transpose(0, 2, 1, 3).reshape(b, l, h * d)


def conv1d_centred(x, w):
    k = w.shape[0]
    p = k // 2
    l = x.shape[1]
    xp = jnp.pad(x, ((0, 0), (p, p), (0, 0)))
    out = xp[:, 0:l] * w[0]
    for i in range(1, k):
        out = out + xp[:, i:i + l] * w[i]
    return out


def dwconv2d_centred(x, w):
    k = w.shape[0]
    p = k // 2
    r, c = x.shape[1], x.shape[2]
    xp = jnp.pad(x, ((0, 0), (p, p), (p, p), (0, 0)))
    out = jnp.zeros_like(x)
    for i in range(k):
        for j in range(k):
            out = out + xp[:, i:i + r, j:j + c] * w[i, j]
    return out


def rope_1d(pos, dim):
    inv = ROPE_BASE ** (-jnp.arange(dim // 2, dtype=jnp.float32) / (dim // 2))
    return pos.astype(jnp.float32)[:, None] * inv[None]


def rope_2d(row, col, dim):
    n = dim // 4
    inv = ROPE_BASE ** (-jnp.arange(n, dtype=jnp.float32) / n)
    return jnp.concatenate([row.astype(jnp.float32)[:, None] * inv[None],
                            col.astype(jnp.float32)[:, None] * inv[None]], -1)


def apply_rotary(x, cos, sin):
    x1, x2 = jnp.split(x, 2, -1)
    return jnp.concatenate([x1 * cos - x2 * sin, x1 * sin + x2 * cos], -1)


def flip_seq(t):
    return jnp.flip(t, axis=2)


def gated_delta_chunked(q, k, v, log_a, beta, s0):
    f32 = jnp.float32
    q, k, v, log_a, beta = (t.astype(f32) for t in (q, k, v, log_a, beta))
    b, h, l, dk = q.shape
    dv = v.shape[-1]
    n = l // CHUNK
    q = q.reshape(b, h, n, CHUNK, dk)
    k = k.reshape(b, h, n, CHUNK, dk)
    v = v.reshape(b, h, n, CHUNK, dv)
    beta = beta.reshape(b, h, n, CHUNK)
    g = jnp.cumsum(log_a.reshape(b, h, n, CHUNK), -1)
    tri = jnp.tril(jnp.ones((CHUNK, CHUNK), bool))
    strict = jnp.tril(jnp.ones((CHUNK, CHUNK), bool), -1)
    dec_incl = jnp.exp(jnp.where(tri, g[..., :, None] - g[..., None, :], -jnp.inf))
    dec_strict = jnp.where(strict, dec_incl, 0.0)
    kb = k * beta[..., None]
    a_mat = jnp.einsum('bhnid,bhnjd->bhnij', kb, k) * dec_strict
    m_mat = a_mat + jnp.eye(CHUNK, dtype=f32)
    rhs = jnp.concatenate([v * beta[..., None], kb * jnp.exp(g)[..., None]], -1)
    sol = lax.linalg.triangular_solve(m_mat, rhs, left_side=True, lower=True, unit_diagonal=True)
    u_c, w_c = sol[..., :dv], sol[..., dv:]
    qk = jnp.einsum('bhnid,bhnjd->bhnij', q, k) * dec_incl
    q_g = q * jnp.exp(g)[..., None]
    k_g = k * jnp.exp(g[..., -1:] - g)[..., None]
    c_dec = jnp.exp(g[..., -1])
    xs = tuple(jnp.moveaxis(t, 2, 0) for t in (u_c, w_c, qk, q_g, k_g, c_dec))

    def step(s, inp):
        u_n, w_n, qk_n, qg_n, kg_n, cd_n = inp
        v_new = u_n - jnp.einsum('bhck,bhkv->bhcv', w_n, s)
        o = jnp.einsum('bhck,bhkv->bhcv', qg_n, s) + jnp.einsum('bhij,bhjv->bhiv', qk_n, v_new)
        s = cd_n[..., None, None] * s + jnp.einsum('bhck,bhcv->bhkv', kg_n, v_new)
        return s, o

    s, o = lax.scan(step, s0.astype(f32), xs)
    return jnp.moveaxis(o, 0, 2).reshape(b, h, l, dv), s


def retention_chunked(q, k, v, log_gamma, s0):
    f32 = jnp.float32
    q, k, v = (t.astype(f32) for t in (q, k, v))
    b, h, l, dk = q.shape
    dv = v.shape[-1]
    n = l // CHUNK
    q = q.reshape(b, h, n, CHUNK, dk)
    k = k.reshape(b, h, n, CHUNK, dk)
    v = v.reshape(b, h, n, CHUNK, dv)
    lg = log_gamma.astype(f32)[:, None]
    idx = jnp.arange(CHUNK, dtype=f32)
    rel = idx[:, None] - idx[None, :]
    dec = jnp.exp(jnp.where(rel >= 0, lg[:, :, None] * rel, -jnp.inf))
    o_in = jnp.einsum('bhnij,bhnjv->bhniv',
                      jnp.einsum('bhnid,bhnjd->bhnij', q, k) * dec[None, :, None], v)
    q_d = q * jnp.exp(lg * (idx + 1))[None, :, None, :, None]
    k_d = k * jnp.exp(lg * (CHUNK - 1 - idx))[None, :, None, :, None]
    c_dec = jnp.exp(lg[:, 0] * CHUNK)[None, :, None, None]
    xs = tuple(jnp.moveaxis(t, 2, 0) for t in (o_in, q_d, k_d, v))

    def step(s, inp):
        o_n, q_n, k_n, v_n = inp
        o = o_n + jnp.einsum('bhck,bhkv->bhcv', q_n, s)
        s = c_dec * s + jnp.einsum('bhck,bhcv->bhkv', k_n, v_n)
        return s, o

    s, o = lax.scan(step, s0.astype(f32), xs)
    return jnp.moveaxis(o, 0, 2).reshape(b, h, l, dv), s


def bidir_delta(q, k, v, log_a, beta, s_init):
    o_f, s_f = gated_delta_chunked(q, k, v, log_a[0], beta[0], s_init[0])
    o_b, s_b = gated_delta_chunked(flip_seq(q), flip_seq(k), flip_seq(v),
                                   flip_seq(log_a[1]), flip_seq(beta[1]), s_init[1])
    return o_f + flip_seq(o_b), (s_f, s_b)


def bidir_retention(q, k, v, log_gamma, s_init):
    o_f, s_f = retention_chunked(q, k, v, log_gamma[0], s_init[0])
    o_b, s_b = retention_chunked(flip_seq(q), flip_seq(k), flip_seq(v), log_gamma[1], s_init[1])
    return o_f + flip_seq(o_b), (s_f, s_b)


def even_mixer(h, hc, w_in, conv_w, a_log, dt_bias, norm_w, ret_decay, w_out, rot, need_ctx):
    f32 = jnp.float32
    splits = np.cumsum(EVEN_SIZES)[:-1].tolist()

    def project(t, rot_t):
        bt, lt = t.shape[:2]
        qkv, z, a_raw, b_raw, rq, rk, rv, rg = jnp.split(t @ w_in, splits, -1)
        qkv = jax.nn.silu(conv1d_centred(qkv, conv_w))
        q, k, v = jnp.split(qkv, [DN_HEADS * DN_DK, 2 * DN_HEADS * DN_DK], -1)
        q = l2_normalize(to_heads(q, DN_HEADS)) * DN_DK ** -0.5
        k = l2_normalize(to_heads(k, DN_HEADS))
        v = to_heads(v, DN_HEADS)
        a_raw = a_raw.reshape(bt, lt, 2, DN_HEADS).transpose(2, 0, 3, 1).astype(f32)
        b_raw = b_raw.reshape(bt, lt, 2, DN_HEADS).transpose(2, 0, 3, 1).astype(f32)
        log_a = -jnp.exp(a_log.astype(f32))[:, None, :, None] * jax.nn.softplus(
            a_raw + dt_bias.astype(f32)[:, None, :, None])
        beta = jax.nn.sigmoid(b_raw)
        rq = rq.reshape(bt, lt, RET_HEADS, RET_DK)
        rk = rk.reshape(bt, lt, RET_HEADS, RET_DK)
        if rot_t is not None:
            rq = apply_rotary(rq, *rot_t)
            rk = apply_rotary(rk, *rot_t)
        rq = rq.transpose(0, 2, 1, 3)
        rk = rk.transpose(0, 2, 1, 3) * RET_DK ** -0.5
        rv = to_heads(rv, RET_HEADS)
        return (q, k, v, log_a, beta), (rq, rk, rv), z, rg

    def finish(o_dn, z, o_ret, g):
        dn = rms_norm(o_dn) * norm_w.astype(f32) * jax.nn.silu(to_heads(z, DN_HEADS).astype(f32))
        ret = merge_heads(group_norm(o_ret)) * jax.nn.silu(g.astype(f32))
        y = jnp.concatenate([merge_heads(dn), ret], -1)
        return y.astype(w_out.dtype) @ w_out

    log_gamma = -jnp.exp(ret_decay.astype(f32))
    dn_c, ret_c, z_c, g_c = project(hc, None)
    dn_l, ret_l, z_l, g_l = project(h, rot)
    bsz = h.shape[0]
    s0_dn = jnp.zeros((bsz, DN_HEADS, DN_DK, DN_DV), f32)
    s0_ret = jnp.zeros((bsz, RET_HEADS, RET_DK, RET_DV), f32)
    o_dn_c, st_dn = bidir_delta(*dn_c, (s0_dn, s0_dn))
    o_ret_c, st_ret = bidir_retention(*ret_c, log_gamma, (s0_ret, s0_ret))
    o_dn_l, _ = bidir_delta(*dn_l, st_dn)
    o_ret_l, _ = bidir_retention(*ret_l, log_gamma, st_ret)
    y = finish(o_dn_l, z_l, o_ret_l, g_l).astype(h.dtype)
    yc = finish(o_dn_c, z_c, o_ret_c, g_c).astype(h.dtype) if need_ctx else None
    return y, yc


def diff_mix(q, k, v, lam):
    s = jnp.einsum('cbhqd,cbhkd->cbhqk', q, k).astype(jnp.float32) * DIFF_HD ** -0.5
    p = jax.nn.softmax(s, -1)
    a = p[0] - lam * p[1]
    return jnp.einsum('bhqk,bhkd->bhqd', a.astype(v.dtype), v)


def odd_mixer(h, hc, w_qkv, lam_p, subln_w, w_out, rot, lambda_init, need_ctx):
    f32 = jnp.float32

    def project(t, rot_t):
        bt, lt = t.shape[:2]
        q, k, v = jnp.split(t @ w_qkv, [DIFF_HEADS * 2 * DIFF_HD, 2 * DIFF_HEADS * 2 * DIFF_HD], -1)
        q = q.reshape(bt, lt, DIFF_HEADS, 2, DIFF_HD)
        k = k.reshape(bt, lt, DIFF_HEADS, 2, DIFF_HD)
        if rot_t is not None:
            q = apply_rotary(q, *rot_t)
            k = apply_rotary(k, *rot_t)
        return q.transpose(3, 0, 2, 1, 4), k.transpose(3, 0, 2, 1, 4), to_heads(v, DIFF_HEADS)

    lp = lam_p.astype(f32)
    lam = jnp.exp(jnp.sum(lp[0] * lp[1])) - jnp.exp(jnp.sum(lp[2] * lp[3])) + lambda_init
    qc, kc, vc = project(hc, None)
    ql, kl, vl = project(h, rot)
    k_all = jnp.concatenate([kl, kc], axis=3)
    v_all = jnp.concatenate([vl, vc], axis=2)
    bsz, nh, l = ql.shape[1], ql.shape[2], ql.shape[3]
    nb = l // Q_BLOCK
    qb = ql.reshape(2, bsz, nh, nb, Q_BLOCK, DIFF_HD).transpose(3, 0, 1, 2, 4, 5)
    ob = lax.map(lambda qq: diff_mix(qq, k_all, v_all, lam), qb)
    o_l = ob.transpose(1, 2, 0, 3, 4).reshape(bsz, nh, l, DIFF_DV)

    def finish(o):
        o = rms_norm(o) * subln_w.astype(f32) * (1.0 - lambda_init)
        return (merge_heads(o).astype(w_out.dtype) @ w_out).astype(h.dtype)

    y = finish(o_l)
    yc = finish(diff_mix(qc, kc, vc, lam)) if need_ctx else None
    return y, yc


def conv_ffn(h, rows, w_gate, w_up, w_conv, w_down):
    bt, lt, _ = h.shape
    a = h @ w_gate
    a = dwconv2d_centred(a.reshape(bt, rows, lt // rows, D_FF), w_conv).reshape(bt, lt, D_FF)
    return (jax.nn.silu(a) * (h @ w_up)) @ w_down


def setup_inputs(seed: int = 0) -> dict:
    key = jax.random.key(seed)
    ks = jax.random.split(key, 24)
    f32 = jnp.float32

    def nrm(k, shape, s):
        return jax.random.normal(k, shape, f32) * s

    x = nrm(ks[0], (BATCH, SEQ, D_MODEL), 1.0)
    c = nrm(ks[1], (BATCH, D_MODEL), 1.0)
    ctx = nrm(ks[2], (BATCH, CTX_LEN, D_MODEL), 1.0)
    c_ctx = nrm(ks[3], (D_MODEL,), 1.0)
    mod_w = nrm(ks[4], (DEPTH, D_MODEL, 6 * D_MODEL), 0.5 * D_MODEL ** -0.5)
    mod_b = nrm(ks[5], (DEPTH, 6 * D_MODEL), 0.02)
    ln_g = 1.0 + nrm(ks[6], (DEPTH, 2, D_MODEL), 0.02)
    ln_b = nrm(ks[7], (DEPTH, 2, D_MODEL), 0.02)
    e_w_in = nrm(ks[8], (N_EVEN, D_MODEL, EVEN_IN), D_MODEL ** -0.5)
    e_conv = nrm(ks[9], (N_EVEN, DN_CONV, DN_QKV), DN_CONV ** -0.5)
    e_a_log = jnp.log(jax.random.uniform(ks[10], (N_EVEN, 2, DN_HEADS), f32, 1.0, 16.0))
    dt = jnp.exp(jax.random.uniform(ks[11], (N_EVEN, 2, DN_HEADS), f32, math.log(1e-3), math.log(1e-1)))
    e_dt_bias = dt + jnp.log(-jnp.expm1(-dt))
    e_norm_w = 1.0 + nrm(ks[12], (N_EVEN, DN_DV), 0.02)
    base = jnp.log(-jnp.log1p(-jnp.power(2.0, -5.0 - jnp.arange(RET_HEADS, dtype=f32))))
    e_ret_decay = base + nrm(ks[13], (N_EVEN, 2, RET_HEADS), 0.05)
    e_w_out = nrm(ks[14], (N_EVEN, EVEN_MIX, D_MODEL), DEEP_BETA * EVEN_MIX ** -0.5)
    o_w_qkv = nrm(ks[15], (N_ODD, D_MODEL, ODD_IN), D_MODEL ** -0.5)
    o_lambda = nrm(ks[16], (N_ODD, 4, DIFF_HD), 0.1)
    o_subln_w = 1.0 + nrm(ks[17], (N_ODD, DIFF_DV), 0.02)
    o_w_out = nrm(ks[18], (N_ODD, ODD_MIX, D_MODEL), DEEP_BETA * ODD_MIX ** -0.5)
    f_w_gate = nrm(ks[19], (DEPTH, D_MODEL, D_FF), D_MODEL ** -0.5)
    f_w_up = nrm(ks[20], (DEPTH, D_MODEL, D_FF), D_MODEL ** -0.5)
    f_conv = nrm(ks[21], (DEPTH, FFN_CONV, FFN_CONV, D_FF), 1.0 / FFN_CONV)
    f_w_down = nrm(ks[22], (DEPTH, D_FF, D_MODEL), DEEP_BETA * D_FF ** -0.5)
    return {'x': x, 'c': c, 'ctx': ctx, 'c_ctx': c_ctx, 'mod_w': mod_w, 'mod_b': mod_b,
            'ln_g': ln_g, 'ln_b': ln_b, 'e_w_in': e_w_in, 'e_conv': e_conv, 'e_a_log': e_a_log,
            'e_dt_bias': e_dt_bias, 'e_norm_w': e_norm_w, 'e_ret_decay': e_ret_decay,
            'e_w_out': e_w_out, 'o_w_qkv': o_w_qkv, 'o_lambda': o_lambda, 'o_subln_w': o_subln_w,
            'o_w_out': o_w_out, 'f_w_gate': f_w_gate, 'f_w_up': f_w_up, 'f_conv': f_conv,
            'f_w_down': f_w_down}


def reference(x, c, ctx, c_ctx, mod_w, mod_b, ln_g, ln_b, e_w_in, e_conv, e_a_log, e_dt_bias,
              e_norm_w, e_ret_decay, e_w_out, o_w_qkv, o_lambda, o_subln_w, o_w_out,
              f_w_gate, f_w_up, f_conv, f_w_down):
    l = x.shape[1]
    rows = l // GRID_W
    pos = jnp.arange(l)
    ret_ang = rope_1d(pos, RET_DK)
    diff_ang = rope_2d(pos // GRID_W, pos % GRID_W, DIFF_HD)
    ret_rot = (jnp.cos(ret_ang)[None, :, None, :].astype(x.dtype),
               jnp.sin(ret_ang)[None, :, None, :].astype(x.dtype))
    diff_rot = (jnp.cos(diff_ang)[None, :, None, None, :].astype(x.dtype),
                jnp.sin(diff_ang)[None, :, None, None, :].astype(x.dtype))
    c_act = jax.nn.silu(c)
    cc_act = jax.nn.silu(c_ctx)
    for li in range(DEPTH):
        last = li == DEPTH - 1
        mod = (c_act @ mod_w[li] + mod_b[li])[:, None, :]
        modc = cc_act @ mod_w[li] + mod_b[li]
        sh_a, sc_a, g_a, sh_f, sc_f, g_f = jnp.split(mod, 6, -1)
        csh_a, csc_a, cg_a, csh_f, csc_f, cg_f = jnp.split(modc, 6, -1)
        h = modulate(x, sh_a, sc_a)
        hc = modulate(ctx, csh_a, csc_a)
        i = li // 2
        if li % 2 == 0:
            y, yc = even_mixer(h, hc, e_w_in[i], e_conv[i], e_a_log[i], e_dt_bias[i], e_norm_w[i],
                               e_ret_decay[i], e_w_out[i], ret_rot, not last)
        else:
            lambda_init = 0.8 - 0.6 * math.exp(-0.3 * li)
            y, yc = odd_mixer(h, hc, o_w_qkv[i], o_lambda[i], o_subln_w[i], o_w_out[i], diff_rot,
                              lambda_init, not last)
        x = layer_norm(DEEP_ALPHA * x + g_a * y, ln_g[li, 0], ln_b[li, 0])
        hf = modulate(x, sh_f, sc_f)
        x = layer_norm(DEEP_ALPHA * x + g_f * conv_ffn(hf, rows, f_w_gate[li], f_w_up[li], f_conv[li], f_w_down[li]),
                       ln_g[li, 1], ln_b[li, 1])
        if not last:
            ctx = layer_norm(DEEP_ALPHA * ctx + cg_a * yc, ln_g[li, 0], ln_b[li, 0])
            hcf = modulate(ctx, csh_f, csc_f)
            ctx = layer_norm(DEEP_ALPHA * ctx + cg_f * conv_ffn(hcf, 1, f_w_gate[li], f_w_up[li], f_conv[li], f_w_down[li]),
                             ln_g[li, 1], ln_b[li, 1])
    return x
```

```python
import functools
import math

import jax
import jax.numpy as jnp
from jax import lax
from jax.experimental import pallas as pl
from jax.experimental.pallas import tpu as pltpu

F32 = jnp.float32
BF16 = jnp.bfloat16

GRID_W = 64
DN_HEADS = 4
RET_HEADS = 4
HEAD_DIM = 128
DIFF_HEADS = 8
DIFF_HD = 64
DIFF_DV = 2 * DIFF_HD
ROPE_BASE = 10000.0
LN_EPS = 1e-5
NORM_EPS = 1e-6

CHUNK = 128
INV_BASE = 16
BF16_ROWS = 16
VMEM_LIMIT = 56 * 1024 * 1024


def _dot(a, b):
    return jnp.dot(a, b, preferred_element_type=F32)


def _silu(x):
    return x * jax.nn.sigmoid(x)


def _softplus(x):
    return jnp.maximum(x, 0.0) + jnp.log1p(jnp.exp(-jnp.abs(x)))


def _cparams(sem):
    return pltpu.CompilerParams(dimension_semantics=sem, vmem_limit_bytes=VMEM_LIMIT)


def _mod_kernel(cc_ref, w_ref, b_ref, o_ref):
    a = _silu(cc_ref[...])
    w = w_ref[0]
    a_hi = a.astype(BF16)
    a_lo = (a - a_hi.astype(F32)).astype(BF16)
    w_hi = w.astype(BF16)
    w_lo = (w - w_hi.astype(F32)).astype(BF16)
    o_ref[0] = _dot(a_hi, w_hi) + _dot(a_hi, w_lo) + _dot(a_lo, w_hi) + b_ref[0]


def _mod_call(cc, mod_w, mod_b):
    depth, d, n = mod_w.shape
    r = cc.shape[0]
    tn = min(n, 1536)
    return pl.pallas_call(
        _mod_kernel,
        grid=(depth, n // tn),
        in_specs=[pl.BlockSpec((r, d), lambda l, j: (0, 0)),
                  pl.BlockSpec((1, d, tn), lambda l, j: (l, 0, j)),
                  pl.BlockSpec((1, 1, tn), lambda l, j: (l, 0, j))],
        out_specs=pl.BlockSpec((1, r, tn), lambda l, j: (l, 0, j)),
        out_shape=jax.ShapeDtypeStruct((depth, r, n), F32),
        compiler_params=_cparams(("parallel", "parallel")),
        name="adaln_maps",
    )(cc, mod_w, mod_b.reshape(depth, 1, n))


def _mod_spec(d, k, is_ctx, nb, ngrid):
    if ngrid == 2:
        imap = (lambda b, i: (nb, 0, k)) if is_ctx else (lambda b, i: (b, 0, k))
    else:
        imap = (lambda b, i, j: (nb, 0, k)) if is_ctx else (lambda b, i, j: (b, 0, k))
    return pl.BlockSpec((1, 1, d), imap)


def _proj_kernel(x_ref, sh_ref, sc_ref, *refs, n_w, tn):
    w_refs, o_refs = refs[:n_w], refs[n_w:]
    h = (x_ref[0] * (1.0 + sc_ref[0]) + sh_ref[0]).astype(BF16)
    for w_ref, o_ref in zip(w_refs, o_refs):
        n = w_ref.shape[1]
        for n0 in range(0, n, tn):
            n1 = min(n0 + tn, n)
            o_ref[0, :, n0:n1] = _dot(h, w_ref[:, n0:n1]).astype(o_ref.dtype)


def _proj_call(x, mod_l, k_shift, is_ctx, weights, out_dtypes, name):
    b, l, d = x.shape
    nb = mod_l.shape[0] - 1
    tm = min(l, 512)
    in_specs = [pl.BlockSpec((1, tm, d), lambda bi, i: (bi, i, 0)),
                _mod_spec(d, k_shift, is_ctx, nb, 2),
                _mod_spec(d, k_shift + 1, is_ctx, nb, 2)]
    out_specs, out_shape = [], []
    for w, dt in zip(weights, out_dtypes):
        n = w.shape[1]
        in_specs.append(pl.BlockSpec((d, n), lambda bi, i: (0, 0)))
        out_specs.append(pl.BlockSpec((1, tm, n), lambda bi, i: (bi, i, 0)))
        out_shape.append(jax.ShapeDtypeStruct((b, l, n), dt))
    return pl.pallas_call(
        functools.partial(_proj_kernel, n_w=len(weights), tn=512),
        grid=(b, l // tm),
        in_specs=in_specs, out_specs=out_specs, out_shape=out_shape,
        compiler_params=_cparams(("parallel", "parallel")),
        name=name,
    )(x, mod_l, mod_l, *weights)


def _outln_kernel(*refs, alpha, has_w, has_h):
    refs = list(refs)
    y_ref, x_ref, g_ref = refs[:3]
    refs = refs[3:]
    w_ref = refs.pop(0) if has_w else None
    lng_ref, lnb_ref = refs[:2]
    refs = refs[2:]
    if has_h:
        sh_ref, sc_ref, o_ref, h_ref = refs
    else:
        (o_ref,) = refs
    t = _dot(y_ref[0], w_ref[...]) if has_w else y_ref[0].astype(F32)
    z = alpha * x_ref[0] + g_ref[0] * t
    mu = jnp.mean(z, axis=-1, keepdims=True)
    zc = z - mu
    var = jnp.mean(zc * zc, axis=-1, keepdims=True)
    o = zc * lax.rsqrt(var + LN_EPS) * lng_ref[...] + lnb_ref[...]
    o_ref[0] = o
    if has_h:
        h_ref[0] = (o * (1.0 + sc_ref[0]) + sh_ref[0]).astype(BF16)


def _outln_call(y, x, mod_l, k_gate, is_ctx, w, ln_g, ln_b, alpha, k_shift_next, name):
    b, l, d = x.shape
    kdim = y.shape[-1]
    nb = mod_l.shape[0] - 1
    tm = min(l, 512)
    has_w, has_h = w is not None, k_shift_next is not None
    args = [y, x, mod_l]
    in_specs = [pl.BlockSpec((1, tm, kdim), lambda bi, i: (bi, i, 0)),
                pl.BlockSpec((1, tm, d), lambda bi, i: (bi, i, 0)),
                _mod_spec(d, k_gate, is_ctx, nb, 2)]
    if has_w:
        args.append(w)
        in_specs.append(pl.BlockSpec((kdim, d), lambda bi, i: (0, 0)))
    args += [ln_g.reshape(1, d), ln_b.reshape(1, d)]
    in_specs += [pl.BlockSpec((1, d), lambda bi, i: (0, 0))] * 2
    out_specs = [pl.BlockSpec((1, tm, d), lambda bi, i: (bi, i, 0))]
    out_shape = [jax.ShapeDtypeStruct((b, l, d), F32)]
    if has_h:
        args += [mod_l, mod_l]
        in_specs += [_mod_spec(d, k_shift_next, is_ctx, nb, 2), _mod_spec(d, k_shift_next + 1, is_ctx, nb, 2)]
        out_specs.append(pl.BlockSpec((1, tm, d), lambda bi, i: (bi, i, 0)))
        out_shape.append(jax.ShapeDtypeStruct((b, l, d), BF16))
    res = pl.pallas_call(
        functools.partial(_outln_kernel, alpha=alpha, has_w=has_w, has_h=has_h),
        grid=(b, l // tm),
        in_specs=in_specs, out_specs=out_specs, out_shape=out_shape,
        compiler_params=_cparams(("parallel", "parallel")),
        name=name,
    )(*args)
    return res if has_h else res[0]


def _ffn_kernel(h_ref, wg_ref, wu_ref, wc_ref, wd_ref, ml_ref, mr_ref, o_ref, apad, acc, *, l, w, pad, rb):
    f = pl.program_id(1)
    tf = wg_ref.shape[1]

    @pl.when(f == 0)
    def _():
        acc[...] = jnp.zeros_like(acc)
        apad[0:pad, :] = jnp.zeros((pad, tf), F32)
        apad[pad + l:pad + l + pad, :] = jnp.zeros((pad, tf), F32)

    h = h_ref[0]
    apad[pad:pad + l, :] = _dot(h, wg_ref[...])
    rows = w > 0 and l // w > 1
    for r0 in range(0, l, rb):
        conv = jnp.zeros((rb, tf), F32)
        for di in ((-1, 0, 1) if rows else (0,)):
            base = pad + r0 + di * w
            left = apad[base - 1:base - 1 + rb, :] * ml_ref[r0:r0 + rb, :]
            mid = apad[base:base + rb, :]
            right = apad[base + 1:base + 1 + rb, :] * mr_ref[r0:r0 + rb, :]
            k0 = (di + 1) * 3
            conv = conv + left * wc_ref[k0:k0 + 1, :] + mid * wc_ref[k0 + 1:k0 + 2, :] + right * wc_ref[k0 + 2:k0 + 3, :]
        u = _dot(h_ref[0, r0:r0 + rb, :], wu_ref[...])
        t = (_silu(conv) * u).astype(BF16)
        acc[r0:r0 + rb, :] += _dot(t, wd_ref[...])

    @pl.when(f == pl.num_programs(1) - 1)
    def _():
        o_ref[0] = acc[...].astype(o_ref.dtype)


def _ffn_call(hf, w_gate, w_up, w_conv, w_down, grid_w, name):
    b, l, d = hf.shape
    dff = w_gate.shape[1]
    tf = 256 if dff % 256 == 0 else 128
    rb = min(l, 256)
    pad = grid_w + 8 if l // grid_w > 1 else 8
    col = jnp.arange(l) % grid_w
    ones = jnp.ones((l, tf), F32)
    m_left = ones * (col != 0)[:, None]
    m_right = ones * (col != grid_w - 1)[:, None]
    return pl.pallas_call(
        functools.partial(_ffn_kernel, l=l, w=grid_w, pad=pad, rb=rb),
        grid=(b, dff // tf),
        in_specs=[pl.BlockSpec((1, l, d), lambda bi, f: (bi, 0, 0)),
                  pl.BlockSpec((d, tf), lambda bi, f: (0, f)),
                  pl.BlockSpec((d, tf), lambda bi, f: (0, f)),
                  pl.BlockSpec((9, tf), lambda bi, f: (0, f)),
                  pl.BlockSpec((tf, d), lambda bi, f: (f, 0)),
                  pl.BlockSpec((l, tf), lambda bi, f: (0, 0)),
                  pl.BlockSpec((l, tf), lambda bi, f: (0, 0))],
        out_specs=pl.BlockSpec((1, l, d), lambda bi, f: (bi, 0, 0)),
        out_shape=jax.ShapeDtypeStruct((b, l, d), BF16),
        scratch_shapes=[pltpu.VMEM((l + 2 * pad, tf), F32), pltpu.VMEM((l, d), F32)],
        compiler_params=_cparams(("parallel", "arbitrary")),
        name=name,
    )(hf, w_gate, w_up, w_conv, w_down, m_left, m_right)


def _cumsum_rows(x, reverse):
    rows = lax.broadcasted_iota(jnp.int32, x.shape, 0)
    s = 1
    while s < CHUNK:
        if reverse:
            x = x + jnp.where(rows < CHUNK - s, pltpu.roll(x, CHUNK - s, 0), 0.0)
        else:
            x = x + jnp.where(rows >= s, pltpu.roll(x, s, 0), 0.0)
        s *= 2
    return x


def _even_kernel(pc_ref, pt_ref, gc_ref, gt_ref, cw_ref, hp_ref, nw_ref, cos_ref, sin_ref,
                 yc_ref, yt_ref,
                 q_s, k_s, v_s, kt_s, kk_s, qk_s, x_s, xt_s, rq_s, rkt_s, rv_s,
                 of_s, ob_s, rf_s, rb_s, *, nc, nt):
    c = CHUNK
    hd = HEAD_DIM
    ntot = nc + nt
    ri = lax.broadcasted_iota(jnp.int32, (c, c), 0)
    ci = lax.broadcasted_iota(jnp.int32, (c, c), 1)
    rif = ri.astype(F32)
    cif = ci.astype(F32)
    eye = (ri == ci).astype(F32)

    def same_block(size):
        sh = size.bit_length() - 1
        return jnp.right_shift(ri, sh) == jnp.right_shift(ci, sh)

    diag_mask = same_block(INV_BASE).astype(F32)
    off_masks = []
    size = INV_BASE
    while size < c:
        off_masks.append(jnp.where(same_block(2 * size), 1.0, 0.0) - jnp.where(same_block(size), 1.0, 0.0))
        size *= 2
    hp = hp_ref[0]
    a_log_row, dt_row = hp[0:1, :], hp[1:2, :]
    lg_f = -jnp.exp(hp[2:3, :])
    lg_b = -jnp.exp(hp[3:4, :])

    def prep(src_ref, gsrc_ref, ns, off, rot):
        ln = ns * c

        def body(m, carry):
            r0 = pl.multiple_of(m * c, c)
            rows = pl.ds(r0, c)
            x = src_ref[0, rows, 0:3 * hd].astype(F32)
            pstart = pl.multiple_of(jnp.maximum(r0 - BF16_ROWS, 0), BF16_ROWS)
            nstart = pl.multiple_of(jnp.minimum(r0 + c, ln - BF16_ROWS), BF16_ROWS)
            prev = src_ref[0, pl.ds(pstart, BF16_ROWS), 0:3 * hd].astype(F32)[BF16_ROWS - 1:BF16_ROWS, :]
            nxt = src_ref[0, pl.ds(nstart, BF16_ROWS), 0:3 * hd].astype(F32)[0:1, :]
            prev = prev * (m > 0).astype(F32)
            nxt = nxt * (m < ns - 1).astype(F32)
            rr = lax.broadcasted_iota(jnp.int32, (c, 3 * hd), 0)
            xm1 = jnp.where(rr == 0, prev, pltpu.roll(x, 1, 0))
            xp1 = jnp.where(rr == c - 1, nxt, pltpu.roll(x, c - 1, 0))
            cw = cw_ref[0]
            y = _silu(xm1 * cw[0:1, :] + x * cw[1:2, :] + xp1 * cw[2:3, :])
            q = y[:, 0:hd]
            k = y[:, hd:2 * hd]
            q = q * lax.rsqrt(jnp.sum(q * q, axis=-1, keepdims=True) + NORM_EPS) * (hd ** -0.5)
            k = k * lax.rsqrt(jnp.sum(k * k, axis=-1, keepdims=True) + NORM_EPS)
            kt = k.T
            n = m + off
            q_s[n] = q
            k_s[n] = k
            v_s[n] = y[:, 2 * hd:3 * hd]
            kt_s[n] = kt
            kb16, ktb16 = k.astype(BF16), kt.astype(BF16)
            kk_s[n] = _dot(kb16, ktb16)
            qk_s[n] = _dot(q.astype(BF16), ktb16)
            g = gsrc_ref[0, rows, :]
            la = -jnp.exp(a_log_row) * _softplus(g + dt_row)
            xg = jnp.where(ci == 0, _cumsum_rows(la, False),
                           jnp.where(ci == 1, _cumsum_rows(la, True), jax.nn.sigmoid(g)))
            x_s[n] = xg
            xt_s[n] = xg.T
            rq = src_ref[0, rows, 4 * hd:5 * hd].astype(F32)
            rk = src_ref[0, rows, 5 * hd:6 * hd].astype(F32)
            if rot:
                cs, sn = cos_ref[rows, :], sin_ref[rows, :]
                rq = rq * cs + pltpu.roll(rq, hd // 2, 1) * sn
                rk = rk * cs + pltpu.roll(rk, hd // 2, 1) * sn
            rq_s[n] = rq
            rkt_s[n] = (rk * (hd ** -0.5)).T
            rv_s[n] = src_ref[0, rows, 6 * hd:7 * hd].astype(F32)
            return carry

        lax.fori_loop(0, ns, body, 0)

    prep(pc_ref, gc_ref, nc, 0, False)
    prep(pt_ref, gt_ref, nt, nc, True)

    dec_bi = (jnp.where(ri >= ci, jnp.exp(lg_f * (rif - cif)), 0.0)
              + jnp.where(ci >= ri, jnp.exp(lg_b * (cif - rif)), 0.0))
    gq_f = jnp.exp(lg_f * (rif + 1.0))
    gq_b = jnp.exp(lg_b * (c - rif))
    gk_f = jnp.exp(lg_f * (c - 1.0 - cif))
    gk_b = jnp.exp(lg_b * cif)
    cd_f = jnp.exp(lg_f * c)
    cd_b = jnp.exp(lg_b * c)

    def delta_chunk(n, d, s):
        xg, xt = x_s[n], xt_s[n]
        g = xg[:, d:d + 1]
        beta = xg[:, 2 + d:3 + d]
        g_row = xt[d:d + 1, :]
        last = 0 if d else c - 1
        g_last = xg[last:last + 1, d:d + 1]
        diff = jnp.minimum(g - g_row, 0.0)
        e = jnp.exp(diff)
        incl = (ri <= ci) if d else (ri >= ci)
        strict = (ri < ci) if d else (ri > ci)
        dec_incl = jnp.where(incl, e, 0.0)
        k, kt, q, v = k_s[n], kt_s[n], q_s[n], v_s[n]
        nm = -(beta * kk_s[n]) * jnp.where(strict, e, 0.0)
        nd = nm * diag_mask
        p = eye + nd
        nd16 = nd.astype(BF16)
        npow = _dot(nd16, nd16)
        lvl = 2
        while lvl * 2 < INV_BASE:
            r = _dot(jnp.concatenate([p, npow], axis=0).astype(BF16), npow.astype(BF16))
            p = p + r[0:c, :]
            npow = r[c:, :]
            lvl *= 2
        p = p + _dot(p.astype(BF16), npow.astype(BF16))
        for om in off_masks:
            p16 = p.astype(BF16)
            p = p + _dot(_dot(p16, (nm * om).astype(BF16)).astype(BF16), p16)
        eg = jnp.exp(g)
        kb = k * beta
        rhs = jnp.concatenate([v * beta, kb * eg], axis=1).astype(BF16)
        sol = _dot(p.astype(BF16), rhs)
        u, w = sol[:, 0:hd], sol[:, hd:2 * hd]
        r1 = _dot(jnp.concatenate([w, q * eg], axis=0).astype(BF16), s.astype(BF16))
        v_new = u - r1[0:c, :]
        kgt = kt * jnp.exp(g_last - g_row)
        r2 = _dot(jnp.concatenate([qk_s[n] * dec_incl, kgt], axis=0).astype(BF16), v_new.astype(BF16))
        o = r1[c:, :] + r2[0:c, :]
        s_new = jnp.exp(g_last) * s + r2[c:, :]
        return o, s_new

    def step(i, carry):
        s_f, s_b, t_f, t_b = carry
        nf = i
        nbk = jnp.where(i < nc, nc - 1 - i, ntot - 1 - (i - nc))
        o, s_f = delta_chunk(nf, 0, s_f)
        of_s[nf] = o
        o, s_b = delta_chunk(nbk, 1, s_b)
        ob_s[nbk] = o
        rq, rkt, rv = rq_s[nf], rkt_s[nf], rv_s[nf].astype(BF16)
        sc = _dot(rq.astype(BF16), rkt.astype(BF16)) * dec_bi
        r = _dot(jnp.concatenate([sc, rkt * gk_f], axis=0).astype(BF16), rv)
        rf_s[nf] = r[0:c, :] + _dot((rq * gq_f).astype(BF16), t_f.astype(BF16))
        t_f = cd_f * t_f + r[c:, :]
        rqb, rktb, rvb = rq_s[nbk], rkt_s[nbk], rv_s[nbk].astype(BF16)
        rb_s[nbk] = _dot((rqb * gq_b).astype(BF16), t_b.astype(BF16))
        t_b = cd_b * t_b + _dot((rktb * gk_b).astype(BF16), rvb)
        return s_f, s_b, t_f, t_b

    z = jnp.zeros((hd, hd), F32)
    lax.fori_loop(0, ntot, step, (z, z, z, z))

    def finish(src_ref, dst_ref, ns, off):
        def body(m, carry):
            rows = pl.ds(pl.multiple_of(m * c, c), c)
            n = m + off
            o = of_s[n] + ob_s[n]
            zg = src_ref[0, rows, 3 * hd:4 * hd].astype(F32)
            dn = o * lax.rsqrt(jnp.mean(o * o, axis=-1, keepdims=True) + NORM_EPS) * nw_ref[...] * _silu(zg)
            r = rf_s[n] + rb_s[n]
            mu = jnp.mean(r, axis=-1, keepdims=True)
            rc = r - mu
            rg = src_ref[0, rows, 7 * hd:8 * hd].astype(F32)
            rt = rc * lax.rsqrt(jnp.mean(rc * rc, axis=-1, keepdims=True) + NORM_EPS) * _silu(rg)
            dst_ref[0, rows, 0:hd] = dn.astype(dst_ref.dtype)
            dst_ref[0, rows, hd:2 * hd] = rt.astype(dst_ref.dtype)
            return carry

        lax.fori_loop(0, ns, body, 0)

    finish(pc_ref, yc_ref, nc, 0)
    finish(pt_ref, yt_ref, nt, nc)


def _even_call(pc, pt, gc, gt, conv_h, hp, norm_w, cos_t, sin_t):
    b, lc, _ = pc.shape
    lt = pt.shape[1]
    nc, nt = lc // CHUNK, lt // CHUNK
    nh = DN_HEADS
    hd = HEAD_DIM
    blk = 8 * hd
    chunk_scr = pltpu.VMEM((nc + nt, CHUNK, CHUNK), F32)
    return pl.pallas_call(
        functools.partial(_even_kernel, nc=nc, nt=nt),
        grid=(b, nh),
        in_specs=[pl.BlockSpec((1, lc, blk), lambda bi, h: (bi, 0, h)),
                  pl.BlockSpec((1, lt, blk), lambda bi, h: (bi, 0, h)),
                  pl.BlockSpec((1, lc, hd), lambda bi, h: (bi, 0, h)),
                  pl.BlockSpec((1, lt, hd), lambda bi, h: (bi, 0, h)),
                  pl.BlockSpec((1, 3, 3 * hd), lambda bi, h: (h, 0, 0)),
                  pl.BlockSpec((1, 8, hd), lambda bi, h: (h, 0, 0)),
                  pl.BlockSpec((1, hd), lambda bi, h: (0, 0)),
                  pl.BlockSpec((lt, hd), lambda bi, h: (0, 0)),
                  pl.BlockSpec((lt, hd), lambda bi, h: (0, 0))],
        out_specs=[pl.BlockSpec((1, lc, 2 * hd), lambda bi, h: (bi, 0, h)),
                   pl.BlockSpec((1, lt, 2 * hd), lambda bi, h: (bi, 0, h))],
        out_shape=[jax.ShapeDtypeStruct((b, lc, nh * 2 * hd), BF16),
                   jax.ShapeDtypeStruct((b, lt, nh * 2 * hd), BF16)],
        scratch_shapes=[chunk_scr] * 15,
        compiler_params=_cparams(("parallel", "arbitrary")),
        name="deltanet_retention",
    )(pc, pt, gc, gt, conv_h, hp, norm_w.reshape(1, hd), cos_t, sin_t)


def _rot2d(x, c_ref, s1_ref, s2_ref):
    return x * c_ref[...] + pltpu.roll(x, 96, 1) * s1_ref[...] + pltpu.roll(x, 32, 1) * s2_ref[...]


def _attn_kernel(q_ref, kt_ref, vt_ref, kc_ref, vc_ref, qc_ref, qs1_ref, qs2_ref, kc_t, ks1_t, ks2_t,
                 lam_ref, sw_ref, o_ref, kfull, vfull, *, lt, lc, lam_init):
    qi = pl.program_id(2)
    hd = DIFF_HD
    lk = lt + lc

    @pl.when(qi == 0)
    def _():
        for r0 in range(0, lt, 128):
            kr = _rot2d(kt_ref[0, r0:r0 + 128, :].astype(F32), kc_t.at[r0:r0 + 128, :],
                        ks1_t.at[r0:r0 + 128, :], ks2_t.at[r0:r0 + 128, :]).T
            kfull[0, :, r0:r0 + 128] = kr[0:hd, :].astype(BF16)
            kfull[1, :, r0:r0 + 128] = kr[hd:2 * hd, :].astype(BF16)
        for r0 in range(0, lc, 128):
            kr = kc_ref[0, r0:r0 + 128, :].astype(F32).T
            kfull[0, :, lt + r0:lt + r0 + 128] = kr[0:hd, :].astype(BF16)
            kfull[1, :, lt + r0:lt + r0 + 128] = kr[hd:2 * hd, :].astype(BF16)
        lane = lax.broadcasted_iota(jnp.int32, (lk, 128), 1)
        vfull[0:lt, 0:128] = vt_ref[0].astype(vfull.dtype)
        vfull[lt:lk, 0:128] = vc_ref[0].astype(vfull.dtype)
        vfull[:, 128:256] = jnp.where(lane == 0, 1.0, 0.0).astype(BF16)

    lp = lam_ref[...]
    lam = (jnp.exp(jnp.sum(lp[0:1, :] * lp[1:2, :], axis=-1, keepdims=True))
           - jnp.exp(jnp.sum(lp[2:3, :] * lp[3:4, :], axis=-1, keepdims=True)) + lam_init)
    q = (_rot2d(q_ref[0].astype(F32), qc_ref, qs1_ref, qs2_ref) * (hd ** -0.5)).astype(BF16)
    outs = []
    for comp in range(2):
        s = _dot(q[:, comp * hd:(comp + 1) * hd], kfull[comp])
        p = jnp.exp(s - jnp.max(s, axis=-1, keepdims=True)).astype(BF16)
        r = _dot(p, vfull[...])
        outs.append(r[:, 0:128] / r[:, 128:129])
    o = outs[0] - lam * outs[1]
    o = o * lax.rsqrt(jnp.mean(o * o, axis=-1, keepdims=True) + NORM_EPS) * sw_ref[...] * (1.0 - lam_init)
    o_ref[0] = o.astype(o_ref.dtype)


def _attn_call(pt, pc, tabs, lam_p, subln_w, lam_init):
    b, lt, _ = pt.shape
    lc = pc.shape[1]
    nh = DIFF_HEADS
    tq = min(lt, 256)
    ct, s1t, s2t = tabs
    tab_q = pl.BlockSpec((tq, 128), lambda bi, h, qi: (qi, 0))
    tab_k = pl.BlockSpec((lt, 128), lambda bi, h, qi: (0, 0))
    return pl.pallas_call(
        functools.partial(_attn_kernel, lt=lt, lc=lc, lam_init=lam_init),
        grid=(b, nh, lt // tq),
        in_specs=[pl.BlockSpec((1, tq, 128), lambda bi, h, qi: (bi, qi, h)),
                  pl.BlockSpec((1, lt, 128), lambda bi, h, qi: (bi, 0, nh + h)),
                  pl.BlockSpec((1, lt, 128), lambda bi, h, qi: (bi, 0, 2 * nh + h)),
                  pl.BlockSpec((1, lc, 128), lambda bi, h, qi: (bi, 0, nh + h)),
                  pl.BlockSpec((1, lc, 128), lambda bi, h, qi: (bi, 0, 2 * nh + h)),
                  tab_q, tab_q, tab_q, tab_k, tab_k, tab_k,
                  pl.BlockSpec((4, DIFF_HD), lambda bi, h, qi: (0, 0)),
                  pl.BlockSpec((1, 128), lambda bi, h, qi: (0, 0))],
        out_specs=pl.BlockSpec((1, tq, 128), lambda bi, h, qi: (bi, qi, h)),
        out_shape=jax.ShapeDtypeStruct((b, lt, nh * DIFF_DV), BF16),
        scratch_shapes=[pltpu.VMEM((2, DIFF_HD, lt + lc), BF16), pltpu.VMEM((lt + lc, 256), BF16)],
        compiler_params=_cparams(("parallel", "parallel", "arbitrary")),
        name="diff_attention",
    )(pt, pt, pt, pc, pc, ct, s1t, s2t, ct, s1t, s2t, lam_p, subln_w.reshape(1, 128))


def _even_weights(w_in, conv_w, a_log, dt_bias, ret_decay, w_out):
    nh, hd = DN_HEADS, HEAD_DIM
    o_z, o_a, o_b = 3 * nh * hd, 4 * nh * hd, 4 * nh * hd + 2 * nh
    o_rq = o_b + 2 * nh

    def col(base, h):
        return w_in[:, base + h * hd:base + (h + 1) * hd]

    main, gate, conv_h, hp = [], [], [], []
    d = w_in.shape[0]
    for h in range(nh):
        main += [col(0, h), col(nh * hd, h), col(2 * nh * hd, h), col(o_z, h),
                 col(o_rq, h), col(o_rq + nh * hd, h), col(o_rq + 2 * nh * hd, h), col(o_rq + 3 * nh * hd, h)]
        gate += [w_in[:, o_a + h:o_a + h + 1], w_in[:, o_a + nh + h:o_a + nh + h + 1],
                 w_in[:, o_b + h:o_b + h + 1], w_in[:, o_b + nh + h:o_b + nh + h + 1],
                 jnp.zeros((d, hd - 4), w_in.dtype)]
        conv_h.append(jnp.concatenate([conv_w[:, h * hd:(h + 1) * hd], conv_w[:, (nh + h) * hd:(nh + h + 1) * hd],
                                       conv_w[:, (2 * nh + h) * hd:(2 * nh + h + 1) * hd]], axis=1))
        lane = jnp.arange(hd)
        a_row = jnp.where(lane == 1, a_log[1, h], a_log[0, h])
        dt_row = jnp.where(lane == 1, dt_bias[1, h], dt_bias[0, h])
        ones = jnp.ones((hd,), F32)
        hp.append(jnp.stack([a_row, dt_row, ones * ret_decay[0, h], ones * ret_decay[1, h]]
                            + [jnp.zeros((hd,), F32)] * 4))
    w_main = jnp.concatenate(main, axis=1).astype(BF16)
    w_gate = jnp.concatenate(gate, axis=1).astype(BF16)
    w_o = jnp.concatenate([w_out[(j * nh + h) * hd:(j * nh + h + 1) * hd] for h in range(nh) for j in range(2)],
                          axis=0).astype(BF16)
    return w_main, w_gate, jnp.stack(conv_h).astype(F32), jnp.stack(hp).astype(F32), w_o


def _ret_tables(l):
    half = HEAD_DIM // 2
    inv = ROPE_BASE ** (-jnp.arange(half, dtype=F32) / half)
    ang = jnp.arange(l).astype(F32)[:, None] * inv[None]
    cs, sn = jnp.cos(ang), jnp.sin(ang)
    return jnp.concatenate([cs, cs], -1), jnp.concatenate([-sn, sn], -1)


def _diff_tables(l):
    n = DIFF_HD // 4
    pos = jnp.arange(l)
    inv = ROPE_BASE ** (-jnp.arange(n, dtype=F32) / n)
    ang = jnp.concatenate([(pos // GRID_W).astype(F32)[:, None] * inv[None],
                           (pos % GRID_W).astype(F32)[:, None] * inv[None]], -1)
    cs, sn, zero = jnp.cos(ang), jnp.sin(ang), jnp.zeros_like(ang)
    c64 = jnp.concatenate([cs, cs], -1)
    s1 = jnp.concatenate([-sn, zero], -1)
    s2 = jnp.concatenate([zero, sn], -1)
    return tuple(jnp.concatenate([t, t], -1) for t in (c64, s1, s2))


def kernel(x, c, ctx, c_ctx, mod_w, mod_b, ln_g, ln_b, e_w_in, e_conv, e_a_log, e_dt_bias, e_norm_w, e_ret_decay,
           e_w_out, o_w_qkv, o_lambda, o_subln_w, o_w_out, f_w_gate, f_w_up, f_conv, f_w_down):
    depth = mod_w.shape[0]
    b, l, d = x.shape
    lc = ctx.shape[1]
    alpha = (2 * depth) ** 0.25
    n_rows = -(-(b + 1) // 8) * 8
    cc = jnp.concatenate([c, c_ctx[None, :], jnp.zeros((n_rows - b - 1, d), F32)], axis=0)
    mod = _mod_call(cc, mod_w, mod_b).reshape(depth, n_rows, 1, 6 * d)[:, :b + 1]
    ret_cos, ret_sin = _ret_tables(l)
    diff_tabs = _diff_tables(l)
    for li in range(depth):
        last = li == depth - 1
        i = li // 2
        mod_l = mod[li]
        if li % 2 == 0:
            w_main, w_gate, conv_h, hp, w_o = _even_weights(e_w_in[i], e_conv[i], e_a_log[i], e_dt_bias[i],
                                                            e_ret_decay[i], e_w_out[i])
            pc, gc = _proj_call(ctx, mod_l, 0, True, [w_main, w_gate], [BF16, F32], "even_proj_ctx")
            pt, gt = _proj_call(x, mod_l, 0, False, [w_main, w_gate], [BF16, F32], "even_proj_lat")
            yc, yt = _even_call(pc, pt, gc, gt, conv_h, hp, e_norm_w[i], ret_cos, ret_sin)
        else:
            lam_init = 0.8 - 0.6 * math.exp(-0.3 * li)
            w_qkv = o_w_qkv[i].astype(BF16)
            (pc,) = _proj_call(ctx, mod_l, 0, True, [w_qkv], [BF16], "odd_proj_ctx")
            (pt,) = _proj_call(x, mod_l, 0, False, [w_qkv], [BF16], "odd_proj_lat")
            yt = _attn_call(pt, pc, diff_tabs, o_lambda[i], o_subln_w[i], lam_init)
            w_o = o_w_out[i].astype(BF16)
            yc = None
            if not last:
                raise NotImplementedError("context update after a differential-attention layer")
        wg, wu, wd = f_w_gate[li].astype(BF16), f_w_up[li].astype(BF16), f_w_down[li].astype(BF16)
        wcv = f_conv[li].reshape(9, -1)
        x1, hf = _outln_call(yt, x, mod_l, 2, False, w_o, ln_g[li, 0], ln_b[li, 0], alpha, 3, "mix_out_lat")
        yf = _ffn_call(hf, wg, wu, wcv, wd, GRID_W, "ffn_lat")
        x = _outln_call(yf, x1, mod_l, 5, False, None, ln_g[li, 1], ln_b[li, 1], alpha, None, "ffn_ln_lat")
        if not last:
            c1, hcf = _outln_call(yc, ctx, mod_l, 2, True, w_o, ln_g[li, 0], ln_b[li, 0], alpha, 3, "mix_out_ctx")
            ycf = _ffn_call(hcf, wg, wu, wcv, wd, lc, "ffn_ctx")
            ctx = _outln_call(ycf, c1, mod_l, 5, True, None, ln_g[li, 1], ln_b[li, 1], alpha, None, "ffn_ln_ctx")
    return x
```

```python
import functools
import math

import jax
import jax.numpy as jnp
from jax import lax
from jax.experimental import pallas as pl
from jax.experimental.pallas import tpu as pltpu

F32 = jnp.float32
BF16 = jnp.bfloat16

GRID_W = 64
DN_HEADS = 4
RET_HEADS = 4
HEAD_DIM = 128
DIFF_HEADS = 8
DIFF_HD = 64
DIFF_DV = 2 * DIFF_HD
ROPE_BASE = 10000.0
LN_EPS = 1e-5
NORM_EPS = 1e-6

CHUNK = 128
INV_BASE = 16
PAR_CHUNKS = 4
BF16_ROWS = 16
VMEM_LIMIT = 56 * 1024 * 1024


def _dot(a, b):
    return jnp.dot(a, b, preferred_element_type=F32)


def _silu(x):
    return x * jax.nn.sigmoid(x)


def _softplus(x):
    return jnp.maximum(x, 0.0) + jnp.log1p(jnp.exp(-jnp.abs(x)))


def _cparams(sem):
    return pltpu.CompilerParams(dimension_semantics=sem, vmem_limit_bytes=VMEM_LIMIT)


def _mod_kernel(cc_ref, w_ref, b_ref, o_ref):
    a = _silu(cc_ref[...])
    w = w_ref[0]
    a_hi = a.astype(BF16)
    a_lo = (a - a_hi.astype(F32)).astype(BF16)
    w_hi = w.astype(BF16)
    w_lo = (w - w_hi.astype(F32)).astype(BF16)
    o_ref[0] = _dot(a_hi, w_hi) + _dot(a_hi, w_lo) + _dot(a_lo, w_hi) + b_ref[0]


def _mod_call(cc, mod_w, mod_b):
    depth, d, n = mod_w.shape
    r = cc.shape[0]
    tn = min(n, 1536)
    return pl.pallas_call(
        _mod_kernel,
        grid=(depth, n // tn),
        in_specs=[pl.BlockSpec((r, d), lambda l, j: (0, 0)),
                  pl.BlockSpec((1, d, tn), lambda l, j: (l, 0, j)),
                  pl.BlockSpec((1, 1, tn), lambda l, j: (l, 0, j))],
        out_specs=pl.BlockSpec((1, r, tn), lambda l, j: (l, 0, j)),
        out_shape=jax.ShapeDtypeStruct((depth, r, n), F32),
        compiler_params=_cparams(("parallel", "parallel")),
        name="adaln_maps",
    )(cc, mod_w, mod_b.reshape(depth, 1, n))


def _mod_spec(d, k, is_ctx, nb, ngrid):
    if ngrid == 2:
        imap = (lambda b, i: (nb, 0, k)) if is_ctx else (lambda b, i: (b, 0, k))
    else:
        imap = (lambda b, i, j: (nb, 0, k)) if is_ctx else (lambda b, i, j: (b, 0, k))
    return pl.BlockSpec((1, 1, d), imap)


def _proj_kernel(x_ref, sh_ref, sc_ref, *refs, n_w, tn):
    w_refs, o_refs = refs[:n_w], refs[n_w:]
    h = (x_ref[0] * (1.0 + sc_ref[0]) + sh_ref[0]).astype(BF16)
    for w_ref, o_ref in zip(w_refs, o_refs):
        n = w_ref.shape[1]
        for n0 in range(0, n, tn):
            n1 = min(n0 + tn, n)
            o_ref[0, :, n0:n1] = _dot(h, w_ref[:, n0:n1]).astype(o_ref.dtype)


def _proj_call(x, mod_l, k_shift, is_ctx, weights, out_dtypes, name):
    b, l, d = x.shape
    nb = mod_l.shape[0] - 1
    tm = min(l, 512)
    in_specs = [pl.BlockSpec((1, tm, d), lambda bi, i: (bi, i, 0)),
                _mod_spec(d, k_shift, is_ctx, nb, 2),
                _mod_spec(d, k_shift + 1, is_ctx, nb, 2)]
    out_specs, out_shape = [], []
    for w, dt in zip(weights, out_dtypes):
        n = w.shape[1]
        in_specs.append(pl.BlockSpec((d, n), lambda bi, i: (0, 0)))
        out_specs.append(pl.BlockSpec((1, tm, n), lambda bi, i: (bi, i, 0)))
        out_shape.append(jax.ShapeDtypeStruct((b, l, n), dt))
    return pl.pallas_call(
        functools.partial(_proj_kernel, n_w=len(weights), tn=512),
        grid=(b, l // tm),
        in_specs=in_specs, out_specs=out_specs, out_shape=out_shape,
        compiler_params=_cparams(("parallel", "parallel")),
        name=name,
    )(x, mod_l, mod_l, *weights)


def _outln_kernel(*refs, alpha, has_w, has_h):
    refs = list(refs)
    y_ref, x_ref, g_ref = refs[:3]
    refs = refs[3:]
    w_ref = refs.pop(0) if has_w else None
    lng_ref, lnb_ref = refs[:2]
    refs = refs[2:]
    if has_h:
        sh_ref, sc_ref, o_ref, h_ref = refs
    else:
        (o_ref,) = refs
    t = _dot(y_ref[0], w_ref[...]) if has_w else y_ref[0].astype(F32)
    z = alpha * x_ref[0] + g_ref[0] * t
    mu = jnp.mean(z, axis=-1, keepdims=True)
    zc = z - mu
    var = jnp.mean(zc * zc, axis=-1, keepdims=True)
    o = zc * lax.rsqrt(var + LN_EPS) * lng_ref[...] + lnb_ref[...]
    o_ref[0] = o
    if has_h:
        h_ref[0] = (o * (1.0 + sc_ref[0]) + sh_ref[0]).astype(BF16)


def _outln_call(y, x, mod_l, k_gate, is_ctx, w, ln_g, ln_b, alpha, k_shift_next, name):
    b, l, d = x.shape
    kdim = y.shape[-1]
    nb = mod_l.shape[0] - 1
    tm = min(l, 512)
    has_w, has_h = w is not None, k_shift_next is not None
    args = [y, x, mod_l]
    in_specs = [pl.BlockSpec((1, tm, kdim), lambda bi, i: (bi, i, 0)),
                pl.BlockSpec((1, tm, d), lambda bi, i: (bi, i, 0)),
                _mod_spec(d, k_gate, is_ctx, nb, 2)]
    if has_w:
        args.append(w)
        in_specs.append(pl.BlockSpec((kdim, d), lambda bi, i: (0, 0)))
    args += [ln_g.reshape(1, d), ln_b.reshape(1, d)]
    in_specs += [pl.BlockSpec((1, d), lambda bi, i: (0, 0))] * 2
    out_specs = [pl.BlockSpec((1, tm, d), lambda bi, i: (bi, i, 0))]
    out_shape = [jax.ShapeDtypeStruct((b, l, d), F32)]
    if has_h:
        args += [mod_l, mod_l]
        in_specs += [_mod_spec(d, k_shift_next, is_ctx, nb, 2), _mod_spec(d, k_shift_next + 1, is_ctx, nb, 2)]
        out_specs.append(pl.BlockSpec((1, tm, d), lambda bi, i: (bi, i, 0)))
        out_shape.append(jax.ShapeDtypeStruct((b, l, d), BF16))
    res = pl.pallas_call(
        functools.partial(_outln_kernel, alpha=alpha, has_w=has_w, has_h=has_h),
        grid=(b, l // tm),
        in_specs=in_specs, out_specs=out_specs, out_shape=out_shape,
        compiler_params=_cparams(("parallel", "parallel")),
        name=name,
    )(*args)
    return res if has_h else res[0]


def _ffn_kernel(h_ref, wg_ref, wu_ref, wc_ref, wd_ref, ml_ref, mr_ref, o_ref, apad, acc, *, l, w, pad, rb):
    f = pl.program_id(1)
    tf = wg_ref.shape[1]

    @pl.when(f == 0)
    def _():
        acc[...] = jnp.zeros_like(acc)
        apad[0:pad, :] = jnp.zeros((pad, tf), F32)
        apad[pad + l:pad + l + pad, :] = jnp.zeros((pad, tf), F32)

    h = h_ref[0]
    apad[pad:pad + l, :] = _dot(h, wg_ref[...])
    rows = w > 0 and l // w > 1
    for r0 in range(0, l, rb):
        conv = jnp.zeros((rb, tf), F32)
        for di in ((-1, 0, 1) if rows else (0,)):
            base = pad + r0 + di * w
            left = apad[base - 1:base - 1 + rb, :] * ml_ref[r0:r0 + rb, :]
            mid = apad[base:base + rb, :]
            right = apad[base + 1:base + 1 + rb, :] * mr_ref[r0:r0 + rb, :]
            k0 = (di + 1) * 3
            conv = conv + left * wc_ref[k0:k0 + 1, :] + mid * wc_ref[k0 + 1:k0 + 2, :] + right * wc_ref[k0 + 2:k0 + 3, :]
        u = _dot(h_ref[0, r0:r0 + rb, :], wu_ref[...])
        t = (_silu(conv) * u).astype(BF16)
        acc[r0:r0 + rb, :] += _dot(t, wd_ref[...])

    @pl.when(f == pl.num_programs(1) - 1)
    def _():
        o_ref[0] = acc[...].astype(o_ref.dtype)


def _ffn_call(hf, w_gate, w_up, w_conv, w_down, grid_w, name):
    b, l, d = hf.shape
    dff = w_gate.shape[1]
    tf = 256 if dff % 256 == 0 else 128
    rb = min(l, 256)
    pad = grid_w + 8 if l // grid_w > 1 else 8
    col = jnp.arange(l) % grid_w
    ones = jnp.ones((l, tf), F32)
    m_left = ones * (col != 0)[:, None]
    m_right = ones * (col != grid_w - 1)[:, None]
    return pl.pallas_call(
        functools.partial(_ffn_kernel, l=l, w=grid_w, pad=pad, rb=rb),
        grid=(b, dff // tf),
        in_specs=[pl.BlockSpec((1, l, d), lambda bi, f: (bi, 0, 0)),
                  pl.BlockSpec((d, tf), lambda bi, f: (0, f)),
                  pl.BlockSpec((d, tf), lambda bi, f: (0, f)),
                  pl.BlockSpec((9, tf), lambda bi, f: (0, f)),
                  pl.BlockSpec((tf, d), lambda bi, f: (f, 0)),
                  pl.BlockSpec((l, tf), lambda bi, f: (0, 0)),
                  pl.BlockSpec((l, tf), lambda bi, f: (0, 0))],
        out_specs=pl.BlockSpec((1, l, d), lambda bi, f: (bi, 0, 0)),
        out_shape=jax.ShapeDtypeStruct((b, l, d), BF16),
        scratch_shapes=[pltpu.VMEM((l + 2 * pad, tf), F32), pltpu.VMEM((l, d), F32)],
        compiler_params=_cparams(("parallel", "arbitrary")),
        name=name,
    )(hf, w_gate, w_up, w_conv, w_down, m_left, m_right)


def _even_kernel(pc_ref, pt_ref, gc_ref, gt_ref, cw_ref, hp_ref, nw_ref, cos_ref, sin_ref,
                 yc_ref, yt_ref,
                 u_s, wq_s, qkk_s, cd_s, ri_s, kvf_s, kvb_s, rqf_s, rqb_s, of_s, ob_s, rb_s, *, nc, nt):
    c = CHUNK
    hd = HEAD_DIM
    ntot = nc + nt
    ri = lax.broadcasted_iota(jnp.int32, (c, c), 0)
    ci = lax.broadcasted_iota(jnp.int32, (c, c), 1)
    rif = ri.astype(F32)
    cif = ci.astype(F32)
    eye = (ri == ci).astype(F32)

    def same_block(size):
        sh = size.bit_length() - 1
        return jnp.right_shift(ri, sh) == jnp.right_shift(ci, sh)

    diag_mask = same_block(INV_BASE).astype(F32)
    off_masks = []
    size = INV_BASE
    while size < c:
        off_masks.append(jnp.where(same_block(2 * size), 1.0, 0.0) - jnp.where(same_block(size), 1.0, 0.0))
        size *= 2
    hp = hp_ref[0]
    a_log8, dt8 = hp[0:8, :], hp[8:16, :]
    lg_f = -jnp.exp(hp[16:17, :])
    lg_b = -jnp.exp(hp[17:18, :])
    dec_bi = (jnp.where(ri >= ci, jnp.exp(lg_f * (rif - cif)), 0.0)
              + jnp.where(ci >= ri, jnp.exp(lg_b * (cif - rif)), 0.0))
    gq_f = jnp.exp(lg_f * (rif + 1.0))
    gq_b = jnp.exp(lg_b * (c - rif))
    gk_f = jnp.exp(lg_f * (c - 1.0 - cif))
    gk_b = jnp.exp(lg_b * cif)
    cd_f = jnp.exp(lg_f * c)
    cd_b = jnp.exp(lg_b * c)
    row8 = lax.broadcasted_iota(jnp.int32, (8, c), 0)
    lane8 = lax.broadcasted_iota(jnp.int32, (8, c), 1)

    def gate_rows(g):
        gt8 = g.T[0:8, :]
        la = -jnp.exp(a_log8) * _softplus(gt8 + dt8)
        pre, suf = la, la
        s = 1
        while s < c:
            pre = pre + jnp.where(lane8 >= s, pltpu.roll(pre, s, 1), 0.0)
            suf = suf + jnp.where(lane8 < c - s, pltpu.roll(suf, c - s, 1), 0.0)
            s *= 2
        cum = jnp.where(row8 == 0, pre, suf)
        ecum = pltpu.roll(jnp.exp(cum), 4, 0)
        return jnp.where(row8 < 2, cum, jnp.where(row8 < 4, jax.nn.sigmoid(gt8), jnp.where(row8 < 6, ecum, 0.0)))

    def chunk_prep(src_ref, gsrc_ref, m, ns, off, rot):
        ln = ns * c
        r0 = pl.multiple_of(m * c, c)
        rows = pl.ds(r0, c)
        x = src_ref[0, rows, 0:3 * hd].astype(F32)
        pstart = pl.multiple_of(jnp.maximum(r0 - BF16_ROWS, 0), BF16_ROWS)
        nstart = pl.multiple_of(jnp.minimum(r0 + c, ln - BF16_ROWS), BF16_ROWS)
        prev = src_ref[0, pl.ds(pstart, BF16_ROWS), 0:3 * hd].astype(F32)[BF16_ROWS - 1:BF16_ROWS, :]
        nxt = src_ref[0, pl.ds(nstart, BF16_ROWS), 0:3 * hd].astype(F32)[0:1, :]
        prev = prev * jnp.where(m > 0, 1.0, 0.0)
        nxt = nxt * jnp.where(m < ns - 1, 1.0, 0.0)
        rr = lax.broadcasted_iota(jnp.int32, (c, 3 * hd), 0)
        xm1 = jnp.where(rr == 0, prev, pltpu.roll(x, 1, 0))
        xp1 = jnp.where(rr == c - 1, nxt, pltpu.roll(x, c - 1, 0))
        cw = cw_ref[0]
        y = _silu(xm1 * cw[0:1, :] + x * cw[1:2, :] + xp1 * cw[2:3, :])
        q = y[:, 0:hd]
        k = y[:, hd:2 * hd]
        q = q * (lax.rsqrt(jnp.sum(q * q, axis=-1, keepdims=True) + NORM_EPS) * (hd ** -0.5))
        k = k * lax.rsqrt(jnp.sum(k * k, axis=-1, keepdims=True) + NORM_EPS)
        xt8 = gate_rows(gsrc_ref[0, rows, :])
        xg = jnp.concatenate([xt8, jnp.zeros((c - 8, c), F32)], axis=0).T
        rq = src_ref[0, rows, 4 * hd:5 * hd].astype(F32)
        rk = src_ref[0, rows, 5 * hd:6 * hd].astype(F32)
        if rot:
            cs, sn = cos_ref[rows, :], sin_ref[rows, :]
            rq = rq * cs + pltpu.roll(rq, hd // 2, 1) * sn
            rk = rk * cs + pltpu.roll(rk, hd // 2, 1) * sn
        return dict(n=m + off, q=q, k=k, kt=k.T, v=y[:, 2 * hd:3 * hd], xt8=xt8, xg=xg,
                    rq=rq, rkt=(rk * (hd ** -0.5)).T, rv=src_ref[0, rows, 6 * hd:7 * hd])

    def chunks_work(preps):
        for pr in preps:
            pr["kt16"] = pr["kt"].astype(BF16)
        kks = [_dot(pr["k"].astype(BF16), pr["kt16"]) for pr in preps]
        qks = [_dot(pr["q"].astype(BF16), pr["kt16"]) for pr in preps]
        scs = [_dot(pr["rq"].astype(BF16), pr["rkt"].astype(BF16)) for pr in preps]
        chains = []
        for pr, kk, qk in zip(preps, kks, qks):
            for d in range(2):
                xg, xt8 = pr["xg"], pr["xt8"]
                g, beta, eg = xg[:, d:d + 1], xg[:, 2 + d:3 + d], xg[:, 4 + d:5 + d]
                g_row = xt8[d:d + 1, :]
                last = 0 if d else c - 1
                g_last = xt8[d:d + 1, last:last + 1]
                e = jnp.exp(jnp.minimum(g - g_row, 0.0))
                incl = (ri <= ci) if d else (ri >= ci)
                strict = (ri < ci) if d else (ri > ci)
                nm = -(beta * kk) * jnp.where(strict, e, 0.0)
                nd = nm * diag_mask
                chains.append(dict(pr=pr, d=d, nm=nm, nd16=nd.astype(BF16), p=eye + nd, beta=beta, eg=eg,
                                   qkm=qk * jnp.where(incl, e, 0.0), kgt=pr["kt"] * jnp.exp(g_last - g_row),
                                   cd=jnp.broadcast_to(jnp.exp(g_last), (8, c))))
        for ch in chains:
            ch["npow"] = _dot(ch["nd16"], ch["nd16"])
        lvl = 2
        while lvl * 2 < INV_BASE:
            rs = [_dot(jnp.concatenate([ch["p"], ch["npow"]], axis=0).astype(BF16), ch["npow"].astype(BF16))
                  for ch in chains]
            for ch, r in zip(chains, rs):
                ch["p"] = ch["p"] + r[0:c, :]
                ch["npow"] = r[c:, :]
            lvl *= 2
        rs = [_dot(ch["p"].astype(BF16), ch["npow"].astype(BF16)) for ch in chains]
        for ch, r in zip(chains, rs):
            ch["p"] = ch["p"] + r
        for om in off_masks:
            xs = [_dot(ch["p"].astype(BF16), (ch["nm"] * om).astype(BF16)) for ch in chains]
            rs = [_dot(x.astype(BF16), ch["p"].astype(BF16)) for ch, x in zip(chains, xs)]
            for ch, r in zip(chains, rs):
                ch["p"] = ch["p"] + r
        sols = [_dot(ch["p"].astype(BF16),
                     jnp.concatenate([ch["pr"]["v"] * ch["beta"], ch["pr"]["k"] * (ch["beta"] * ch["eg"])],
                                     axis=1).astype(BF16)) for ch in chains]
        rets = [_dot(jnp.concatenate([sc * dec_bi, pr["rkt"] * gk_f, pr["rkt"] * gk_b], axis=0).astype(BF16), pr["rv"])
                for pr, sc in zip(preps, scs)]
        for ch, sol in zip(chains, sols):
            idx = ch["d"] * ntot + ch["pr"]["n"]
            u_s[idx] = sol[:, 0:hd]
            wq_s[idx] = jnp.concatenate([sol[:, hd:2 * hd], ch["pr"]["q"] * ch["eg"]], axis=0).astype(BF16)
            qkk_s[idx] = jnp.concatenate([ch["qkm"], ch["kgt"]], axis=0).astype(BF16)
            cd_s[idx] = ch["cd"]
        for pr, r in zip(preps, rets):
            n = pr["n"]
            ri_s[n] = r[0:c, :]
            kvf_s[n] = r[c:2 * c, :]
            kvb_s[n] = r[2 * c:3 * c, :]
            rqf_s[n] = (pr["rq"] * gq_f).astype(BF16)
            rqb_s[n] = (pr["rq"] * gq_b).astype(BF16)

    def segment(src_ref, gsrc_ref, ns, off, rot):
        par = math.gcd(ns, PAR_CHUNKS)

        def body(j, carry):
            chunks_work([chunk_prep(src_ref, gsrc_ref, j * par + t, ns, off, rot) for t in range(par)])
            return carry

        lax.fori_loop(0, ns // par, body, 0)

    segment(pc_ref, gc_ref, nc, 0, False)
    segment(pt_ref, gt_ref, nt, nc, True)

    def step(i, carry):
        s_f, s_b, t_f, t_b = carry
        nf = i
        nbk = jnp.where(i < nc, nc - 1 - i, ntot - 1 - (i - nc))
        idxs = (nf, ntot + nbk)
        r1s = [_dot(wq_s[idx], st.astype(BF16)) for idx, st in zip(idxs, (s_f, s_b))]
        r_f = ri_s[nf] + _dot(rqf_s[nf], t_f.astype(BF16))
        r_b = _dot(rqb_s[nbk], t_b.astype(BF16))
        r2s = [_dot(qkk_s[idx], (u_s[idx] - r1[0:c, :]).astype(BF16)) for idx, r1 in zip(idxs, r1s)]
        o_f, o_b = [r1[c:, :] + r2[0:c, :] for r1, r2 in zip(r1s, r2s)]
        s_f, s_b = [cd_s[idx][0:1, :] * st + r2[c:, :] for idx, st, r2 in zip(idxs, (s_f, s_b), r2s)]
        t_f = cd_f * t_f + kvf_s[nf]
        t_b = cd_b * t_b + kvb_s[nbk]
        of_s[nf] = o_f
        ob_s[nbk] = o_b
        ri_s[nf] = r_f
        rb_s[nbk] = r_b
        return s_f, s_b, t_f, t_b

    z = jnp.zeros((hd, hd), F32)
    lax.fori_loop(0, ntot, step, (z, z, z, z))

    def finish(src_ref, dst_ref, ns, off):
        def body(m, carry):
            rows = pl.ds(pl.multiple_of(m * c, c), c)
            n = m + off
            o = of_s[n] + ob_s[n]
            zg = src_ref[0, rows, 3 * hd:4 * hd].astype(F32)
            dn = o * lax.rsqrt(jnp.mean(o * o, axis=-1, keepdims=True) + NORM_EPS) * nw_ref[...] * _silu(zg)
            r = ri_s[n] + rb_s[n]
            mu = jnp.mean(r, axis=-1, keepdims=True)
            rc = r - mu
            rg = src_ref[0, rows, 7 * hd:8 * hd].astype(F32)
            rt = rc * lax.rsqrt(jnp.mean(rc * rc, axis=-1, keepdims=True) + NORM_EPS) * _silu(rg)
            dst_ref[0, rows, 0:hd] = dn.astype(dst_ref.dtype)
            dst_ref[0, rows, hd:2 * hd] = rt.astype(dst_ref.dtype)
            return carry

        lax.fori_loop(0, ns, body, 0)

    finish(pc_ref, yc_ref, nc, 0)
    finish(pt_ref, yt_ref, nt, nc)


def _even_call(pc, pt, gc, gt, conv_h, hp, norm_w, cos_t, sin_t):
    b, lc, _ = pc.shape
    lt = pt.shape[1]
    nc, nt = lc // CHUNK, lt // CHUNK
    ntot = nc + nt
    nh = DN_HEADS
    hd = HEAD_DIM
    blk = 8 * hd
    f32_scr = pltpu.VMEM((ntot, CHUNK, CHUNK), F32)
    bf16_scr = pltpu.VMEM((ntot, CHUNK, CHUNK), BF16)
    return pl.pallas_call(
        functools.partial(_even_kernel, nc=nc, nt=nt),
        grid=(b, nh),
        in_specs=[pl.BlockSpec((1, lc, blk), lambda bi, h: (bi, 0, h)),
                  pl.BlockSpec((1, lt, blk), lambda bi, h: (bi, 0, h)),
                  pl.BlockSpec((1, lc, hd), lambda bi, h: (bi, 0, h)),
                  pl.BlockSpec((1, lt, hd), lambda bi, h: (bi, 0, h)),
                  pl.BlockSpec((1, 3, 3 * hd), lambda bi, h: (h, 0, 0)),
                  pl.BlockSpec((1, 24, hd), lambda bi, h: (h, 0, 0)),
                  pl.BlockSpec((1, hd), lambda bi, h: (0, 0)),
                  pl.BlockSpec((lt, hd), lambda bi, h: (0, 0)),
                  pl.BlockSpec((lt, hd), lambda bi, h: (0, 0))],
        out_specs=[pl.BlockSpec((1, lc, 2 * hd), lambda bi, h: (bi, 0, h)),
                   pl.BlockSpec((1, lt, 2 * hd), lambda bi, h: (bi, 0, h))],
        out_shape=[jax.ShapeDtypeStruct((b, lc, nh * 2 * hd), BF16),
                   jax.ShapeDtypeStruct((b, lt, nh * 2 * hd), BF16)],
        scratch_shapes=[pltpu.VMEM((2 * ntot, CHUNK, CHUNK), F32),
                        pltpu.VMEM((2 * ntot, 2 * CHUNK, CHUNK), BF16),
                        pltpu.VMEM((2 * ntot, 2 * CHUNK, CHUNK), BF16),
                        pltpu.VMEM((2 * ntot, 8, CHUNK), F32),
                        f32_scr, f32_scr, f32_scr,
                        bf16_scr, bf16_scr,
                        f32_scr, f32_scr, f32_scr],
        compiler_params=_cparams(("parallel", "arbitrary")),
        name="deltanet_retention",
    )(pc, pt, gc, gt, conv_h, hp, norm_w.reshape(1, hd), cos_t, sin_t)


def _rot2d(x, c_ref, s1_ref, s2_ref):
    return x * c_ref[...] + pltpu.roll(x, 96, 1) * s1_ref[...] + pltpu.roll(x, 32, 1) * s2_ref[...]


def _attn_kernel(q_ref, kt_ref, vt_ref, kc_ref, vc_ref, qc_ref, qs1_ref, qs2_ref, kc_t, ks1_t, ks2_t,
                 lam_ref, sw_ref, o_ref, kfull, vfull, *, lt, lc, lam_init):
    qi = pl.program_id(2)
    hd = DIFF_HD
    lk = lt + lc

    @pl.when(qi == 0)
    def _():
        for r0 in range(0, lt, 128):
            kr = _rot2d(kt_ref[0, r0:r0 + 128, :].astype(F32), kc_t.at[r0:r0 + 128, :],
                        ks1_t.at[r0:r0 + 128, :], ks2_t.at[r0:r0 + 128, :]).T
            kfull[0, :, r0:r0 + 128] = kr[0:hd, :].astype(BF16)
            kfull[1, :, r0:r0 + 128] = kr[hd:2 * hd, :].astype(BF16)
        for r0 in range(0, lc, 128):
            kr = kc_ref[0, r0:r0 + 128, :].astype(F32).T
            kfull[0, :, lt + r0:lt + r0 + 128] = kr[0:hd, :].astype(BF16)
            kfull[1, :, lt + r0:lt + r0 + 128] = kr[hd:2 * hd, :].astype(BF16)
        lane = lax.broadcasted_iota(jnp.int32, (lk, 128), 1)
        vfull[0:lt, 0:128] = vt_ref[0].astype(vfull.dtype)
        vfull[lt:lk, 0:128] = vc_ref[0].astype(vfull.dtype)
        vfull[:, 128:256] = jnp.where(lane == 0, 1.0, 0.0).astype(BF16)

    lp = lam_ref[...]
    lam = (jnp.exp(jnp.sum(lp[0:1, :] * lp[1:2, :], axis=-1, keepdims=True))
           - jnp.exp(jnp.sum(lp[2:3, :] * lp[3:4, :], axis=-1, keepdims=True)) + lam_init)
    q = (_rot2d(q_ref[0].astype(F32), qc_ref, qs1_ref, qs2_ref) * (hd ** -0.5)).astype(BF16)
    outs = []
    for comp in range(2):
        s = _dot(q[:, comp * hd:(comp + 1) * hd], kfull[comp])
        p = jnp.exp(s - jnp.max(s, axis=-1, keepdims=True)).astype(BF16)
        r = _dot(p, vfull[...])
        outs.append(r[:, 0:128] / r[:, 128:129])
    o = outs[0] - lam * outs[1]
    o = o * lax.rsqrt(jnp.mean(o * o, axis=-1, keepdims=True) + NORM_EPS) * sw_ref[...] * (1.0 - lam_init)
    o_ref[0] = o.astype(o_ref.dtype)


def _attn_call(pt, pc, tabs, lam_p, subln_w, lam_init):
    b, lt, _ = pt.shape
    lc = pc.shape[1]
    nh = DIFF_HEADS
    tq = min(lt, 256)
    ct, s1t, s2t = tabs
    tab_q = pl.BlockSpec((tq, 128), lambda bi, h, qi: (qi, 0))
    tab_k = pl.BlockSpec((lt, 128), lambda bi, h, qi: (0, 0))
    return pl.pallas_call(
        functools.partial(_attn_kernel, lt=lt, lc=lc, lam_init=lam_init),
        grid=(b, nh, lt // tq),
        in_specs=[pl.BlockSpec((1, tq, 128), lambda bi, h, qi: (bi, qi, h)),
                  pl.BlockSpec((1, lt, 128), lambda bi, h, qi: (bi, 0, nh + h)),
                  pl.BlockSpec((1, lt, 128), lambda bi, h, qi: (bi, 0, 2 * nh + h)),
                  pl.BlockSpec((1, lc, 128), lambda bi, h, qi: (bi, 0, nh + h)),
                  pl.BlockSpec((1, lc, 128), lambda bi, h, qi: (bi, 0, 2 * nh + h)),
                  tab_q, tab_q, tab_q, tab_k, tab_k, tab_k,
                  pl.BlockSpec((4, DIFF_HD), lambda bi, h, qi: (0, 0)),
                  pl.BlockSpec((1, 128), lambda bi, h, qi: (0, 0))],
        out_specs=pl.BlockSpec((1, tq, 128), lambda bi, h, qi: (bi, qi, h)),
        out_shape=jax.ShapeDtypeStruct((b, lt, nh * DIFF_DV), BF16),
        scratch_shapes=[pltpu.VMEM((2, DIFF_HD, lt + lc), BF16), pltpu.VMEM((lt + lc, 256), BF16)],
        compiler_params=_cparams(("parallel", "parallel", "arbitrary")),
        name="diff_attention",
    )(pt, pt, pt, pc, pc, ct, s1t, s2t, ct, s1t, s2t, lam_p, subln_w.reshape(1, 128))


def _even_weights(w_in, conv_w, a_log, dt_bias, ret_decay, w_out):
    nh, hd = DN_HEADS, HEAD_DIM
    o_z, o_a, o_b = 3 * nh * hd, 4 * nh * hd, 4 * nh * hd + 2 * nh
    o_rq = o_b + 2 * nh

    def col(base, h):
        return w_in[:, base + h * hd:base + (h + 1) * hd]

    main, gate, conv_h, hp = [], [], [], []
    d = w_in.shape[0]
    for h in range(nh):
        main += [col(0, h), col(nh * hd, h), col(2 * nh * hd, h), col(o_z, h),
                 col(o_rq, h), col(o_rq + nh * hd, h), col(o_rq + 2 * nh * hd, h), col(o_rq + 3 * nh * hd, h)]
        gate += [w_in[:, o_a + h:o_a + h + 1], w_in[:, o_a + nh + h:o_a + nh + h + 1],
                 w_in[:, o_b + h:o_b + h + 1], w_in[:, o_b + nh + h:o_b + nh + h + 1],
                 jnp.zeros((d, hd - 4), w_in.dtype)]
        conv_h.append(jnp.concatenate([conv_w[:, h * hd:(h + 1) * hd], conv_w[:, (nh + h) * hd:(nh + h + 1) * hd],
                                       conv_w[:, (2 * nh + h) * hd:(2 * nh + h + 1) * hd]], axis=1))
        ones = jnp.ones((hd,), F32)
        zero = jnp.zeros((hd,), F32)
        hp.append(jnp.stack([ones * a_log[0, h], ones * a_log[1, h]] + [zero] * 6
                            + [ones * dt_bias[0, h], ones * dt_bias[1, h]] + [zero] * 6
                            + [ones * ret_decay[0, h], ones * ret_decay[1, h]] + [zero] * 6))
    w_main = jnp.concatenate(main, axis=1).astype(BF16)
    w_gate = jnp.concatenate(gate, axis=1).astype(BF16)
    w_o = jnp.concatenate([w_out[(j * nh + h) * hd:(j * nh + h + 1) * hd] for h in range(nh) for j in range(2)],
                          axis=0).astype(BF16)
    return w_main, w_gate, jnp.stack(conv_h).astype(F32), jnp.stack(hp).astype(F32), w_o


def _ret_tables(l):
    half = HEAD_DIM // 2
    inv = ROPE_BASE ** (-jnp.arange(half, dtype=F32) / half)
    ang = jnp.arange(l).astype(F32)[:, None] * inv[None]
    cs, sn = jnp.cos(ang), jnp.sin(ang)
    return jnp.concatenate([cs, cs], -1), jnp.concatenate([-sn, sn], -1)


def _diff_tables(l):
    n = DIFF_HD // 4
    pos = jnp.arange(l)
    inv = ROPE_BASE ** (-jnp.arange(n, dtype=F32) / n)
    ang = jnp.concatenate([(pos // GRID_W).astype(F32)[:, None] * inv[None],
                           (pos % GRID_W).astype(F32)[:, None] * inv[None]], -1)
    cs, sn, zero = jnp.cos(ang), jnp.sin(ang), jnp.zeros_like(ang)
    c64 = jnp.concatenate([cs, cs], -1)
    s1 = jnp.concatenate([-sn, zero], -1)
    s2 = jnp.concatenate([zero, sn], -1)
    return tuple(jnp.concatenate([t, t], -1) for t in (c64, s1, s2))


def kernel(x, c, ctx, c_ctx, mod_w, mod_b, ln_g, ln_b, e_w_in, e_conv, e_a_log, e_dt_bias, e_norm_w, e_ret_decay,
           e_w_out, o_w_qkv, o_lambda, o_subln_w, o_w_out, f_w_gate, f_w_up, f_conv, f_w_down):
    depth = mod_w.shape[0]
    b, l, d = x.shape
    lc = ctx.shape[1]
    alpha = (2 * depth) ** 0.25
    n_rows = -(-(b + 1) // 8) * 8
    cc = jnp.concatenate([c, c_ctx[None, :], jnp.zeros((n_rows - b - 1, d), F32)], axis=0)
    mod = _mod_call(cc, mod_w, mod_b).reshape(depth, n_rows, 1, 6 * d)[:, :b + 1]
    ret_cos, ret_sin = _ret_tables(l)
    diff_tabs = _diff_tables(l)
    for li in range(depth):
        last = li == depth - 1
        i = li // 2
        mod_l = mod[li]
        if li % 2 == 0:
            w_main, w_gate, conv_h, hp, w_o = _even_weights(e_w_in[i], e_conv[i], e_a_log[i], e_dt_bias[i],
                                                            e_ret_decay[i], e_w_out[i])
            pc, gc = _proj_call(ctx, mod_l, 0, True, [w_main, w_gate], [BF16, F32], "even_proj_ctx")
            pt, gt = _proj_call(x, mod_l, 0, False, [w_main, w_gate], [BF16, F32], "even_proj_lat")
            yc, yt = _even_call(pc, pt, gc, gt, conv_h, hp, e_norm_w[i], ret_cos, ret_sin)
        else:
            lam_init = 0.8 - 0.6 * math.exp(-0.3 * li)
            w_qkv = o_w_qkv[i].astype(BF16)
            (pc,) = _proj_call(ctx, mod_l, 0, True, [w_qkv], [BF16], "odd_proj_ctx")
            (pt,) = _proj_call(x, mod_l, 0, False, [w_qkv], [BF16], "odd_proj_lat")
            yt = _attn_call(pt, pc, diff_tabs, o_lambda[i], o_subln_w[i], lam_init)
            w_o = o_w_out[i].astype(BF16)
            yc = None
            if not last:
                raise NotImplementedError("context update after a differential-attention layer")
        wg, wu, wd = f_w_gate[li].astype(BF16), f_w_up[li].astype(BF16), f_w_down[li].astype(BF16)
        wcv = f_conv[li].reshape(9, -1)
        x1, hf = _outln_call(yt, x, mod_l, 2, False, w_o, ln_g[li, 0], ln_b[li, 0], alpha, 3, "mix_out_lat")
        yf = _ffn_call(hf, wg, wu, wcv, wd, GRID_W, "ffn_lat")
        x = _outln_call(yf, x1, mod_l, 5, False, None, ln_g[li, 1], ln_b[li, 1], alpha, None, "ffn_ln_lat")
        if not last:
            c1, hcf = _outln_call(yc, ctx, mod_l, 2, True, w_o, ln_g[li, 0], ln_b[li, 0], alpha, 3, "mix_out_ctx")
            ycf = _ffn_call(hcf, wg, wu, wcv, wd, lc, "ffn_ctx")
            ctx = _outln_call(ycf, c1, mod_l, 5, True, None, ln_g[li, 1], ln_b[li, 1], alpha, None, "ffn_ln_ctx")
    return x
```

```python
import functools
import math

import jax
import jax.numpy as jnp
from jax import lax
from jax.experimental import pallas as pl
from jax.experimental.pallas import tpu as pltpu

F32 = jnp.float32
BF16 = jnp.bfloat16

GRID_W = 64
DN_HEADS = 4
RET_HEADS = 4
HEAD_DIM = 128
DIFF_HEADS = 8
DIFF_HD = 64
DIFF_DV = 2 * DIFF_HD
ROPE_BASE = 10000.0
LN_EPS = 1e-5
NORM_EPS = 1e-6

CHUNK = 128
INV_BASE = 16
PAR_CHUNKS = 4
ATTN_TQ = 512
ATTN_SUB = 128
SCORE_AHEAD = 3
LOG2_E = 1.4426950408889634
GATE_AHEAD = 2
BF16_ROWS = 16
VMEM_LIMIT = 56 * 1024 * 1024


def _dot(a, b):
    return jnp.dot(a, b, preferred_element_type=F32)


def _silu(x):
    return x * jax.nn.sigmoid(x)


def _softplus(x):
    return jnp.maximum(x, 0.0) + jnp.log1p(jnp.exp(-jnp.abs(x)))


def _cparams(sem):
    return pltpu.CompilerParams(dimension_semantics=sem, vmem_limit_bytes=VMEM_LIMIT)


def _mod_kernel(cc_ref, w_ref, b_ref, o_ref):
    a = _silu(cc_ref[...])
    w = w_ref[0]
    a_hi = a.astype(BF16)
    a_lo = (a - a_hi.astype(F32)).astype(BF16)
    w_hi = w.astype(BF16)
    w_lo = (w - w_hi.astype(F32)).astype(BF16)
    o_ref[0] = _dot(a_hi, w_hi) + _dot(a_hi, w_lo) + _dot(a_lo, w_hi) + b_ref[0]


def _mod_call(cc, mod_w, mod_b):
    depth, d, n = mod_w.shape
    r = cc.shape[0]
    tn = min(n, 1536)
    return pl.pallas_call(
        _mod_kernel,
        grid=(depth, n // tn),
        in_specs=[pl.BlockSpec((r, d), lambda l, j: (0, 0)),
                  pl.BlockSpec((1, d, tn), lambda l, j: (l, 0, j)),
                  pl.BlockSpec((1, 1, tn), lambda l, j: (l, 0, j))],
        out_specs=pl.BlockSpec((1, r, tn), lambda l, j: (l, 0, j)),
        out_shape=jax.ShapeDtypeStruct((depth, r, n), F32),
        compiler_params=_cparams(("parallel", "parallel")),
        name="adaln_maps",
    )(cc, mod_w, mod_b.reshape(depth, 1, n))


def _mod_spec(d, k, is_ctx, nb, ngrid):
    if ngrid == 2:
        imap = (lambda b, i: (nb, 0, k)) if is_ctx else (lambda b, i: (b, 0, k))
    else:
        imap = (lambda b, i, j: (nb, 0, k)) if is_ctx else (lambda b, i, j: (b, 0, k))
    return pl.BlockSpec((1, 1, d), imap)


def _proj_kernel(x_ref, sh_ref, sc_ref, *refs, n_w, tn):
    w_refs, o_refs = refs[:n_w], refs[n_w:]
    h = (x_ref[0] * (1.0 + sc_ref[0]) + sh_ref[0]).astype(BF16)
    for w_ref, o_ref in zip(w_refs, o_refs):
        n = w_ref.shape[1]
        for n0 in range(0, n, tn):
            n1 = min(n0 + tn, n)
            o_ref[0, :, n0:n1] = _dot(h, w_ref[:, n0:n1]).astype(o_ref.dtype)


def _proj_call(x, mod_l, k_shift, is_ctx, weights, out_dtypes, name):
    b, l, d = x.shape
    nb = mod_l.shape[0] - 1
    tm = min(l, 512)
    in_specs = [pl.BlockSpec((1, tm, d), lambda bi, i: (bi, i, 0)),
                _mod_spec(d, k_shift, is_ctx, nb, 2),
                _mod_spec(d, k_shift + 1, is_ctx, nb, 2)]
    out_specs, out_shape = [], []
    for w, dt in zip(weights, out_dtypes):
        n = w.shape[1]
        in_specs.append(pl.BlockSpec((d, n), lambda bi, i: (0, 0)))
        out_specs.append(pl.BlockSpec((1, tm, n), lambda bi, i: (bi, i, 0)))
        out_shape.append(jax.ShapeDtypeStruct((b, l, n), dt))
    return pl.pallas_call(
        functools.partial(_proj_kernel, n_w=len(weights), tn=512),
        grid=(b, l // tm),
        in_specs=in_specs, out_specs=out_specs, out_shape=out_shape,
        compiler_params=_cparams(("parallel", "parallel")),
        name=name,
    )(x, mod_l, mod_l, *weights)


def _outln_kernel(*refs, alpha, has_w, has_h):
    refs = list(refs)
    y_ref, x_ref, g_ref = refs[:3]
    refs = refs[3:]
    w_ref = refs.pop(0) if has_w else None
    lng_ref, lnb_ref = refs[:2]
    refs = refs[2:]
    if has_h:
        sh_ref, sc_ref, o_ref, h_ref = refs
    else:
        (o_ref,) = refs
    t = _dot(y_ref[0], w_ref[...]) if has_w else y_ref[0].astype(F32)
    z = alpha * x_ref[0] + g_ref[0] * t
    mu = jnp.mean(z, axis=-1, keepdims=True)
    zc = z - mu
    var = jnp.mean(zc * zc, axis=-1, keepdims=True)
    o = zc * lax.rsqrt(var + LN_EPS) * lng_ref[...] + lnb_ref[...]
    o_ref[0] = o
    if has_h:
        h_ref[0] = (o * (1.0 + sc_ref[0]) + sh_ref[0]).astype(BF16)


def _outln_call(y, x, mod_l, k_gate, is_ctx, w, ln_g, ln_b, alpha, k_shift_next, name):
    b, l, d = x.shape
    kdim = y.shape[-1]
    nb = mod_l.shape[0] - 1
    tm = min(l, 512)
    has_w, has_h = w is not None, k_shift_next is not None
    args = [y, x, mod_l]
    in_specs = [pl.BlockSpec((1, tm, kdim), lambda bi, i: (bi, i, 0)),
                pl.BlockSpec((1, tm, d), lambda bi, i: (bi, i, 0)),
                _mod_spec(d, k_gate, is_ctx, nb, 2)]
    if has_w:
        args.append(w)
        in_specs.append(pl.BlockSpec((kdim, d), lambda bi, i: (0, 0)))
    args += [ln_g.reshape(1, d), ln_b.reshape(1, d)]
    in_specs += [pl.BlockSpec((1, d), lambda bi, i: (0, 0))] * 2
    out_specs = [pl.BlockSpec((1, tm, d), lambda bi, i: (bi, i, 0))]
    out_shape = [jax.ShapeDtypeStruct((b, l, d), F32)]
    if has_h:
        args += [mod_l, mod_l]
        in_specs += [_mod_spec(d, k_shift_next, is_ctx, nb, 2), _mod_spec(d, k_shift_next + 1, is_ctx, nb, 2)]
        out_specs.append(pl.BlockSpec((1, tm, d), lambda bi, i: (bi, i, 0)))
        out_shape.append(jax.ShapeDtypeStruct((b, l, d), BF16))
    res = pl.pallas_call(
        functools.partial(_outln_kernel, alpha=alpha, has_w=has_w, has_h=has_h),
        grid=(b, l // tm),
        in_specs=in_specs, out_specs=out_specs, out_shape=out_shape,
        compiler_params=_cparams(("parallel", "parallel")),
        name=name,
    )(*args)
    return res if has_h else res[0]


def _ffn_kernel(h_ref, wg_ref, wu_ref, wc_ref, wd_ref, ml_ref, mr_ref, o_ref, apad, acc, *, l, w, pad, rb):
    f = pl.program_id(1)
    tf = wg_ref.shape[1]

    @pl.when(f == 0)
    def _():
        acc[...] = jnp.zeros_like(acc)
        apad[0:pad, :] = jnp.zeros((pad, tf), F32)
        apad[pad + l:pad + l + pad, :] = jnp.zeros((pad, tf), F32)

    nblk = l // rb
    dis = (-1, 0, 1) if l // w > 1 else (0,)
    wrow = [wc_ref[k:k + 1, :] for k in range(9)]

    def gate(i):
        apad[pad + i * rb:pad + (i + 1) * rb, :] = _dot(h_ref[0, i * rb:(i + 1) * rb, :], wg_ref[...])

    def colsum(dj, lo, n):
        tot = None
        for di in dis:
            term = apad[lo + di * w:lo + di * w + n, :] * wrow[(di + 1) * 3 + dj + 1]
            tot = term if tot is None else tot + term
        return tot

    for i in range(min(GATE_AHEAD, nblk)):
        gate(i)
    for i in range(nblk):
        if i + GATE_AHEAD < nblk:
            gate(i + GATE_AHEAD)
        r0 = i * rb
        base = pad + r0
        u = _dot(h_ref[0, r0:r0 + rb, :], wu_ref[...])
        conv = (colsum(0, base, rb)
                + colsum(-1, base - 8, rb + 16)[7:7 + rb, :] * ml_ref[r0:r0 + rb, :]
                + colsum(1, base - 8, rb + 16)[9:9 + rb, :] * mr_ref[r0:r0 + rb, :])
        t = (_silu(conv) * u).astype(BF16)
        acc[r0:r0 + rb, :] += _dot(t, wd_ref[...])

    @pl.when(f == pl.num_programs(1) - 1)
    def _():
        o_ref[0] = acc[...].astype(o_ref.dtype)


def _ffn_call(hf, w_gate, w_up, w_conv, w_down, grid_w, name):
    b, l, d = hf.shape
    dff = w_gate.shape[1]
    tf = 256 if dff % 256 == 0 else 128
    rb = min(l, 256)
    pad = grid_w + 8 if l // grid_w > 1 else 8
    col = jnp.arange(l) % grid_w
    ones = jnp.ones((l, tf), F32)
    m_left = ones * (col != 0)[:, None]
    m_right = ones * (col != grid_w - 1)[:, None]
    return pl.pallas_call(
        functools.partial(_ffn_kernel, l=l, w=grid_w, pad=pad, rb=rb),
        grid=(b, dff // tf),
        in_specs=[pl.BlockSpec((1, l, d), lambda bi, f: (bi, 0, 0)),
                  pl.BlockSpec((d, tf), lambda bi, f: (0, f)),
                  pl.BlockSpec((d, tf), lambda bi, f: (0, f)),
                  pl.BlockSpec((9, tf), lambda bi, f: (0, f)),
                  pl.BlockSpec((tf, d), lambda bi, f: (f, 0)),
                  pl.BlockSpec((l, tf), lambda bi, f: (0, 0)),
                  pl.BlockSpec((l, tf), lambda bi, f: (0, 0))],
        out_specs=pl.BlockSpec((1, l, d), lambda bi, f: (bi, 0, 0)),
        out_shape=jax.ShapeDtypeStruct((b, l, d), BF16),
        scratch_shapes=[pltpu.VMEM((l + 2 * pad, tf), F32), pltpu.VMEM((l, d), F32)],
        compiler_params=_cparams(("parallel", "arbitrary")),
        name=name,
    )(hf, w_gate, w_up, w_conv, w_down, m_left, m_right)


def _even_kernel(pc_ref, pt_ref, gc_ref, gt_ref, cw_ref, hp_ref, nw_ref, cos_ref, sin_ref,
                 yc_ref, yt_ref,
                 u_s, wq_s, qkk_s, cd_s, ri_s, kvf_s, kvb_s, rqf_s, rqb_s, of_s, ob_s, rb_s, *, nc, nt):
    c = CHUNK
    hd = HEAD_DIM
    ntot = nc + nt
    ri = lax.broadcasted_iota(jnp.int32, (c, c), 0)
    ci = lax.broadcasted_iota(jnp.int32, (c, c), 1)
    rif = ri.astype(F32)
    cif = ci.astype(F32)
    eye = (ri == ci).astype(F32)

    def same_block(size):
        sh = size.bit_length() - 1
        return jnp.right_shift(ri, sh) == jnp.right_shift(ci, sh)

    diag_mask = same_block(INV_BASE).astype(F32)
    off_masks = []
    size = INV_BASE
    while size < c:
        off_masks.append(jnp.where(same_block(2 * size), 1.0, 0.0) - jnp.where(same_block(size), 1.0, 0.0))
        size *= 2
    hp = hp_ref[0]
    a_log8, dt8 = hp[0:8, :], hp[8:16, :]
    lg_f = -jnp.exp(hp[16:17, :])
    lg_b = -jnp.exp(hp[17:18, :])
    dec_bi = (jnp.where(ri >= ci, jnp.exp(lg_f * (rif - cif)), 0.0)
              + jnp.where(ci >= ri, jnp.exp(lg_b * (cif - rif)), 0.0))
    gq_f = jnp.exp(lg_f * (rif + 1.0))
    gq_b = jnp.exp(lg_b * (c - rif))
    gk_f = jnp.exp(lg_f * (c - 1.0 - cif))
    gk_b = jnp.exp(lg_b * cif)
    cd_f = jnp.exp(lg_f * c)
    cd_b = jnp.exp(lg_b * c)
    row8 = lax.broadcasted_iota(jnp.int32, (8, c), 0)
    lane8 = lax.broadcasted_iota(jnp.int32, (8, c), 1)

    def gate_rows(g):
        gt8 = g.T[0:8, :]
        la = -jnp.exp(a_log8) * _softplus(gt8 + dt8)
        pre, suf = la, la
        s = 1
        while s < c:
            pre = pre + jnp.where(lane8 >= s, pltpu.roll(pre, s, 1), 0.0)
            suf = suf + jnp.where(lane8 < c - s, pltpu.roll(suf, c - s, 1), 0.0)
            s *= 2
        cum = jnp.where(row8 == 0, pre, suf)
        ecum = pltpu.roll(jnp.exp(cum), 4, 0)
        return jnp.where(row8 < 2, cum, jnp.where(row8 < 4, jax.nn.sigmoid(gt8), jnp.where(row8 < 6, ecum, 0.0)))

    def chunk_prep(src_ref, gsrc_ref, m, ns, off, rot):
        ln = ns * c
        r0 = pl.multiple_of(m * c, c)
        rows = pl.ds(r0, c)
        x = src_ref[0, rows, 0:3 * hd].astype(F32)
        pstart = pl.multiple_of(jnp.maximum(r0 - BF16_ROWS, 0), BF16_ROWS)
        nstart = pl.multiple_of(jnp.minimum(r0 + c, ln - BF16_ROWS), BF16_ROWS)
        prev = src_ref[0, pl.ds(pstart, BF16_ROWS), 0:3 * hd].astype(F32)[BF16_ROWS - 1:BF16_ROWS, :]
        nxt = src_ref[0, pl.ds(nstart, BF16_ROWS), 0:3 * hd].astype(F32)[0:1, :]
        prev = prev * jnp.where(m > 0, 1.0, 0.0)
        nxt = nxt * jnp.where(m < ns - 1, 1.0, 0.0)
        rr = lax.broadcasted_iota(jnp.int32, (c, 3 * hd), 0)
        xm1 = jnp.where(rr == 0, prev, pltpu.roll(x, 1, 0))
        xp1 = jnp.where(rr == c - 1, nxt, pltpu.roll(x, c - 1, 0))
        cw = cw_ref[0]
        y = _silu(xm1 * cw[0:1, :] + x * cw[1:2, :] + xp1 * cw[2:3, :])
        q = y[:, 0:hd]
        k = y[:, hd:2 * hd]
        q = q * (lax.rsqrt(jnp.sum(q * q, axis=-1, keepdims=True) + NORM_EPS) * (hd ** -0.5))
        k = k * lax.rsqrt(jnp.sum(k * k, axis=-1, keepdims=True) + NORM_EPS)
        xt8 = gate_rows(gsrc_ref[0, rows, :])
        xg = jnp.concatenate([xt8, jnp.zeros((c - 8, c), F32)], axis=0).T
        rq = src_ref[0, rows, 4 * hd:5 * hd].astype(F32)
        rk = src_ref[0, rows, 5 * hd:6 * hd].astype(F32)
        if rot:
            cs, sn = cos_ref[rows, :], sin_ref[rows, :]
            rq = rq * cs + pltpu.roll(rq, hd // 2, 1) * sn
            rk = rk * cs + pltpu.roll(rk, hd // 2, 1) * sn
        return dict(n=m + off, q=q, k=k, kt=k.T, v=y[:, 2 * hd:3 * hd], xt8=xt8, xg=xg,
                    rq=rq, rkt=(rk * (hd ** -0.5)).T, rv=src_ref[0, rows, 6 * hd:7 * hd])

    def chunks_work(preps):
        for pr in preps:
            pr["kt16"] = pr["kt"].astype(BF16)
        kks = [_dot(pr["k"].astype(BF16), pr["kt16"]) for pr in preps]
        qks = [_dot(pr["q"].astype(BF16), pr["kt16"]) for pr in preps]
        scs = [_dot(pr["rq"].astype(BF16), pr["rkt"].astype(BF16)) for pr in preps]
        chains = []
        for pr, kk, qk in zip(preps, kks, qks):
            for d in range(2):
                xg, xt8 = pr["xg"], pr["xt8"]
                g, beta, eg = xg[:, d:d + 1], xg[:, 2 + d:3 + d], xg[:, 4 + d:5 + d]
                g_row = xt8[d:d + 1, :]
                last = 0 if d else c - 1
                g_last = xt8[d:d + 1, last:last + 1]
                e = jnp.exp(jnp.minimum(g - g_row, 0.0))
                incl = (ri <= ci) if d else (ri >= ci)
                strict = (ri < ci) if d else (ri > ci)
                nm = -(beta * kk) * jnp.where(strict, e, 0.0)
                nd = nm * diag_mask
                chains.append(dict(pr=pr, d=d, nm=nm, nd16=nd.astype(BF16), p=eye + nd, beta=beta, eg=eg,
                                   qkm=qk * jnp.where(incl, e, 0.0), kgt=pr["kt"] * jnp.exp(g_last - g_row),
                                   cd=jnp.broadcast_to(jnp.exp(g_last), (8, c))))
        for ch in chains:
            ch["npow"] = _dot(ch["nd16"], ch["nd16"])
        lvl = 2
        while lvl * 2 < INV_BASE:
            rs = [_dot(jnp.concatenate([ch["p"], ch["npow"]], axis=0).astype(BF16), ch["npow"].astype(BF16))
                  for ch in chains]
            for ch, r in zip(chains, rs):
                ch["p"] = ch["p"] + r[0:c, :]
                ch["npow"] = r[c:, :]
            lvl *= 2
        rs = [_dot(ch["p"].astype(BF16), ch["npow"].astype(BF16)) for ch in chains]
        for ch, r in zip(chains, rs):
            ch["p"] = ch["p"] + r
        for om in off_masks:
            xs = [_dot(ch["p"].astype(BF16), (ch["nm"] * om).astype(BF16)) for ch in chains]
            rs = [_dot(x.astype(BF16), ch["p"].astype(BF16)) for ch, x in zip(chains, xs)]
            for ch, r in zip(chains, rs):
                ch["p"] = ch["p"] + r
        sols = [_dot(ch["p"].astype(BF16),
                     jnp.concatenate([ch["pr"]["v"] * ch["beta"], ch["pr"]["k"] * (ch["beta"] * ch["eg"])],
                                     axis=1).astype(BF16)) for ch in chains]
        rets = [_dot(jnp.concatenate([sc * dec_bi, pr["rkt"] * gk_f, pr["rkt"] * gk_b], axis=0).astype(BF16), pr["rv"])
                for pr, sc in zip(preps, scs)]
        for ch, sol in zip(chains, sols):
            idx = ch["d"] * ntot + ch["pr"]["n"]
            u_s[idx] = sol[:, 0:hd]
            wq_s[idx] = jnp.concatenate([sol[:, hd:2 * hd], ch["pr"]["q"] * ch["eg"]], axis=0).astype(BF16)
            qkk_s[idx] = jnp.concatenate([ch["qkm"], ch["kgt"]], axis=0).astype(BF16)
            cd_s[idx] = ch["cd"]
        for pr, r in zip(preps, rets):
            n = pr["n"]
            ri_s[n] = r[0:c, :]
            kvf_s[n] = r[c:2 * c, :]
            kvb_s[n] = r[2 * c:3 * c, :]
            rqf_s[n] = (pr["rq"] * gq_f).astype(BF16)
            rqb_s[n] = (pr["rq"] * gq_b).astype(BF16)

    def segment(src_ref, gsrc_ref, ns, off, rot):
        par = math.gcd(ns, PAR_CHUNKS)

        def body(j, carry):
            chunks_work([chunk_prep(src_ref, gsrc_ref, j * par + t, ns, off, rot) for t in range(par)])
            return carry

        lax.fori_loop(0, ns // par, body, 0)

    segment(pc_ref, gc_ref, nc, 0, False)
    segment(pt_ref, gt_ref, nt, nc, True)

    def step(i, carry):
        s_f, s_b, t_f, t_b = carry
        nf = i
        nbk = jnp.where(i < nc, nc - 1 - i, ntot - 1 - (i - nc))
        idxs = (nf, ntot + nbk)
        r1s = [_dot(wq_s[idx], st.astype(BF16)) for idx, st in zip(idxs, (s_f, s_b))]
        r_f = ri_s[nf] + _dot(rqf_s[nf], t_f.astype(BF16))
        r_b = _dot(rqb_s[nbk], t_b.astype(BF16))
        r2s = [_dot(qkk_s[idx], (u_s[idx] - r1[0:c, :]).astype(BF16)) for idx, r1 in zip(idxs, r1s)]
        o_f, o_b = [r1[c:, :] + r2[0:c, :] for r1, r2 in zip(r1s, r2s)]
        s_f, s_b = [cd_s[idx][0:1, :] * st + r2[c:, :] for idx, st, r2 in zip(idxs, (s_f, s_b), r2s)]
        t_f = cd_f * t_f + kvf_s[nf]
        t_b = cd_b * t_b + kvb_s[nbk]
        of_s[nf] = o_f
        ob_s[nbk] = o_b
        ri_s[nf] = r_f
        rb_s[nbk] = r_b
        return s_f, s_b, t_f, t_b

    z = jnp.zeros((hd, hd), F32)
    lax.fori_loop(0, ntot, step, (z, z, z, z))

    def finish(src_ref, dst_ref, ns, off):
        def body(m, carry):
            rows = pl.ds(pl.multiple_of(m * c, c), c)
            n = m + off
            o = of_s[n] + ob_s[n]
            zg = src_ref[0, rows, 3 * hd:4 * hd].astype(F32)
            dn = o * lax.rsqrt(jnp.mean(o * o, axis=-1, keepdims=True) + NORM_EPS) * nw_ref[...] * _silu(zg)
            r = ri_s[n] + rb_s[n]
            mu = jnp.mean(r, axis=-1, keepdims=True)
            rc = r - mu
            rg = src_ref[0, rows, 7 * hd:8 * hd].astype(F32)
            rt = rc * lax.rsqrt(jnp.mean(rc * rc, axis=-1, keepdims=True) + NORM_EPS) * _silu(rg)
            dst_ref[0, rows, 0:hd] = dn.astype(dst_ref.dtype)
            dst_ref[0, rows, hd:2 * hd] = rt.astype(dst_ref.dtype)
            return carry

        lax.fori_loop(0, ns, body, 0)

    finish(pc_ref, yc_ref, nc, 0)
    finish(pt_ref, yt_ref, nt, nc)


def _even_call(pc, pt, gc, gt, conv_h, hp, norm_w, cos_t, sin_t):
    b, lc, _ = pc.shape
    lt = pt.shape[1]
    nc, nt = lc // CHUNK, lt // CHUNK
    ntot = nc + nt
    nh = DN_HEADS
    hd = HEAD_DIM
    blk = 8 * hd
    f32_scr = pltpu.VMEM((ntot, CHUNK, CHUNK), F32)
    bf16_scr = pltpu.VMEM((ntot, CHUNK, CHUNK), BF16)
    return pl.pallas_call(
        functools.partial(_even_kernel, nc=nc, nt=nt),
        grid=(b, nh),
        in_specs=[pl.BlockSpec((1, lc, blk), lambda bi, h: (bi, 0, h)),
                  pl.BlockSpec((1, lt, blk), lambda bi, h: (bi, 0, h)),
                  pl.BlockSpec((1, lc, hd), lambda bi, h: (bi, 0, h)),
                  pl.BlockSpec((1, lt, hd), lambda bi, h: (bi, 0, h)),
                  pl.BlockSpec((1, 3, 3 * hd), lambda bi, h: (h, 0, 0)),
                  pl.BlockSpec((1, 24, hd), lambda bi, h: (h, 0, 0)),
                  pl.BlockSpec((1, hd), lambda bi, h: (0, 0)),
                  pl.BlockSpec((lt, hd), lambda bi, h: (0, 0)),
                  pl.BlockSpec((lt, hd), lambda bi, h: (0, 0))],
        out_specs=[pl.BlockSpec((1, lc, 2 * hd), lambda bi, h: (bi, 0, h)),
                   pl.BlockSpec((1, lt, 2 * hd), lambda bi, h: (bi, 0, h))],
        out_shape=[jax.ShapeDtypeStruct((b, lc, nh * 2 * hd), BF16),
                   jax.ShapeDtypeStruct((b, lt, nh * 2 * hd), BF16)],
        scratch_shapes=[pltpu.VMEM((2 * ntot, CHUNK, CHUNK), F32),
                        pltpu.VMEM((2 * ntot, 2 * CHUNK, CHUNK), BF16),
                        pltpu.VMEM((2 * ntot, 2 * CHUNK, CHUNK), BF16),
                        pltpu.VMEM((2 * ntot, 8, CHUNK), F32),
                        f32_scr, f32_scr, f32_scr,
                        bf16_scr, bf16_scr,
                        f32_scr, f32_scr, f32_scr],
        compiler_params=_cparams(("parallel", "arbitrary")),
        name="deltanet_retention",
    )(pc, pt, gc, gt, conv_h, hp, norm_w.reshape(1, hd), cos_t, sin_t)


def _rot2d(x, c_ref, s1_ref, s2_ref):
    return x * c_ref[...] + pltpu.roll(x, 96, 1) * s1_ref[...] + pltpu.roll(x, 32, 1) * s2_ref[...]


def _attn_kernel(q_ref, kt_ref, vt_ref, kc_ref, vc_ref, qc_ref, qs1_ref, qs2_ref, kc_t, ks1_t, ks2_t,
                 lam_ref, sw_ref, o_ref, kfull, vfull, *, lt, lc, lam_init):
    qi = pl.program_id(2)
    hd = DIFF_HD
    lk = lt + lc

    @pl.when(qi == 0)
    def _():
        for r0 in range(0, lt, 128):
            kr = _rot2d(kt_ref[0, r0:r0 + 128, :].astype(F32), kc_t.at[r0:r0 + 128, :],
                        ks1_t.at[r0:r0 + 128, :], ks2_t.at[r0:r0 + 128, :]).T
            kfull[0, :, r0:r0 + 128] = kr[0:hd, :].astype(BF16)
            kfull[1, :, r0:r0 + 128] = kr[hd:2 * hd, :].astype(BF16)
        for r0 in range(0, lc, 128):
            kr = kc_ref[0, r0:r0 + 128, :].astype(F32).T
            kfull[0, :, lt + r0:lt + r0 + 128] = kr[0:hd, :].astype(BF16)
            kfull[1, :, lt + r0:lt + r0 + 128] = kr[hd:2 * hd, :].astype(BF16)
        lane = lax.broadcasted_iota(jnp.int32, (lk, 128), 1)
        vfull[0:lt, 0:128] = vt_ref[0].astype(vfull.dtype)
        vfull[lt:lk, 0:128] = vc_ref[0].astype(vfull.dtype)
        vfull[:, 128:256] = jnp.where(lane == 0, 1.0, 0.0).astype(BF16)

    lp = lam_ref[...]
    lam = (jnp.exp(jnp.sum(lp[0:1, :] * lp[1:2, :], axis=-1, keepdims=True))
           - jnp.exp(jnp.sum(lp[2:3, :] * lp[3:4, :], axis=-1, keepdims=True)) + lam_init)
    q = (_rot2d(q_ref[0].astype(F32), qc_ref, qs1_ref, qs2_ref) * (hd ** -0.5 * LOG2_E)).astype(BF16)
    tq = q.shape[0]
    units = [(r0, comp) for r0 in range(0, tq, ATTN_SUB) for comp in range(2)]

    def score(r0, comp):
        return _dot(q[r0:r0 + ATTN_SUB, comp * hd:(comp + 1) * hd], kfull[comp])

    def weighted(s):
        p = jnp.exp2(s - jnp.max(s, axis=-1, keepdims=True)).astype(BF16)
        r = _dot(p, vfull[...])
        return r[:, 0:128] / r[:, 128:129]

    scores = [score(*u) for u in units[:SCORE_AHEAD]]
    outs = []
    for i in range(len(units)):
        if i + SCORE_AHEAD < len(units):
            scores.append(score(*units[i + SCORE_AHEAD]))
        outs.append(weighted(scores[i]))
        scores[i] = None
    for j, r0 in enumerate(range(0, tq, ATTN_SUB)):
        o = outs[2 * j] - lam * outs[2 * j + 1]
        o = o * lax.rsqrt(jnp.mean(o * o, axis=-1, keepdims=True) + NORM_EPS) * sw_ref[...] * (1.0 - lam_init)
        o_ref[0, r0:r0 + ATTN_SUB, :] = o.astype(o_ref.dtype)


def _attn_call(pt, pc, tabs, lam_p, subln_w, lam_init):
    b, lt, _ = pt.shape
    lc = pc.shape[1]
    nh = DIFF_HEADS
    tq = min(lt, ATTN_TQ)
    ct, s1t, s2t = tabs
    tab_q = pl.BlockSpec((tq, 128), lambda bi, h, qi: (qi, 0))
    tab_k = pl.BlockSpec((lt, 128), lambda bi, h, qi: (0, 0))
    return pl.pallas_call(
        functools.partial(_attn_kernel, lt=lt, lc=lc, lam_init=lam_init),
        grid=(b, nh, lt // tq),
        in_specs=[pl.BlockSpec((1, tq, 128), lambda bi, h, qi: (bi, qi, h)),
                  pl.BlockSpec((1, lt, 128), lambda bi, h, qi: (bi, 0, nh + h)),
                  pl.BlockSpec((1, lt, 128), lambda bi, h, qi: (bi, 0, 2 * nh + h)),
                  pl.BlockSpec((1, lc, 128), lambda bi, h, qi: (bi, 0, nh + h)),
                  pl.BlockSpec((1, lc, 128), lambda bi, h, qi: (bi, 0, 2 * nh + h)),
                  tab_q, tab_q, tab_q, tab_k, tab_k, tab_k,
                  pl.BlockSpec((4, DIFF_HD), lambda bi, h, qi: (0, 0)),
                  pl.BlockSpec((1, 128), lambda bi, h, qi: (0, 0))],
        out_specs=pl.BlockSpec((1, tq, 128), lambda bi, h, qi: (bi, qi, h)),
        out_shape=jax.ShapeDtypeStruct((b, lt, nh * DIFF_DV), BF16),
        scratch_shapes=[pltpu.VMEM((2, DIFF_HD, lt + lc), BF16), pltpu.VMEM((lt + lc, 256), BF16)],
        compiler_params=_cparams(("parallel", "parallel", "arbitrary")),
        name="diff_attention",
    )(pt, pt, pt, pc, pc, ct, s1t, s2t, ct, s1t, s2t, lam_p, subln_w.reshape(1, 128))


def _even_weights(w_in, conv_w, a_log, dt_bias, ret_decay, w_out):
    nh, hd = DN_HEADS, HEAD_DIM
    o_z, o_a, o_b = 3 * nh * hd, 4 * nh * hd, 4 * nh * hd + 2 * nh
    o_rq = o_b + 2 * nh

    def col(base, h):
        return w_in[:, base + h * hd:base + (h + 1) * hd]

    main, gate, conv_h, hp = [], [], [], []
    d = w_in.shape[0]
    for h in range(nh):
        main += [col(0, h), col(nh * hd, h), col(2 * nh * hd, h), col(o_z, h),
                 col(o_rq, h), col(o_rq + nh * hd, h), col(o_rq + 2 * nh * hd, h), col(o_rq + 3 * nh * hd, h)]
        gate += [w_in[:, o_a + h:o_a + h + 1], w_in[:, o_a + nh + h:o_a + nh + h + 1],
                 w_in[:, o_b + h:o_b + h + 1], w_in[:, o_b + nh + h:o_b + nh + h + 1],
                 jnp.zeros((d, hd - 4), w_in.dtype)]
        conv_h.append(jnp.concatenate([conv_w[:, h * hd:(h + 1) * hd], conv_w[:, (nh + h) * hd:(nh + h + 1) * hd],
                                       conv_w[:, (2 * nh + h) * hd:(2 * nh + h + 1) * hd]], axis=1))
        ones = jnp.ones((hd,), F32)
        zero = jnp.zeros((hd,), F32)
        hp.append(jnp.stack([ones * a_log[0, h], ones * a_log[1, h]] + [zero] * 6
                            + [ones * dt_bias[0, h], ones * dt_bias[1, h]] + [zero] * 6
                            + [ones * ret_decay[0, h], ones * ret_decay[1, h]] + [zero] * 6))
    w_main = jnp.concatenate(main, axis=1).astype(BF16)
    w_gate = jnp.concatenate(gate, axis=1).astype(BF16)
    w_o = jnp.concatenate([w_out[(j * nh + h) * hd:(j * nh + h + 1) * hd] for h in range(nh) for j in range(2)],
                          axis=0).astype(BF16)
    return w_main, w_gate, jnp.stack(conv_h).astype(F32), jnp.stack(hp).astype(F32), w_o


def _ret_tables(l):
    half = HEAD_DIM // 2
    inv = ROPE_BASE ** (-jnp.arange(half, dtype=F32) / half)
    ang = jnp.arange(l).astype(F32)[:, None] * inv[None]
    cs, sn = jnp.cos(ang), jnp.sin(ang)
    return jnp.concatenate([cs, cs], -1), jnp.concatenate([-sn, sn], -1)


def _diff_tables(l):
    n = DIFF_HD // 4
    pos = jnp.arange(l)
    inv = ROPE_BASE ** (-jnp.arange(n, dtype=F32) / n)
    ang = jnp.concatenate([(pos // GRID_W).astype(F32)[:, None] * inv[None],
                           (pos % GRID_W).astype(F32)[:, None] * inv[None]], -1)
    cs, sn, zero = jnp.cos(ang), jnp.sin(ang), jnp.zeros_like(ang)
    c64 = jnp.concatenate([cs, cs], -1)
    s1 = jnp.concatenate([-sn, zero], -1)
    s2 = jnp.concatenate([zero, sn], -1)
    return tuple(jnp.concatenate([t, t], -1) for t in (c64, s1, s2))


def kernel(x, c, ctx, c_ctx, mod_w, mod_b, ln_g, ln_b, e_w_in, e_conv, e_a_log, e_dt_bias, e_norm_w, e_ret_decay,
           e_w_out, o_w_qkv, o_lambda, o_subln_w, o_w_out, f_w_gate, f_w_up, f_conv, f_w_down):
    depth = mod_w.shape[0]
    b, l, d = x.shape
    lc = ctx.shape[1]
    alpha = (2 * depth) ** 0.25
    n_rows = -(-(b + 1) // 8) * 8
    cc = jnp.concatenate([c, c_ctx[None, :], jnp.zeros((n_rows - b - 1, d), F32)], axis=0)
    mod = _mod_call(cc, mod_w, mod_b).reshape(depth, n_rows, 1, 6 * d)[:, :b + 1]
    ret_cos, ret_sin = _ret_tables(l)
    diff_tabs = _diff_tables(l)
    for li in range(depth):
        last = li == depth - 1
        i = li // 2
        mod_l = mod[li]
        if li % 2 == 0:
            w_main, w_gate, conv_h, hp, w_o = _even_weights(e_w_in[i], e_conv[i], e_a_log[i], e_dt_bias[i],
                                                            e_ret_decay[i], e_w_out[i])
            pc, gc = _proj_call(ctx, mod_l, 0, True, [w_main, w_gate], [BF16, F32], "even_proj_ctx")
            pt, gt = _proj_call(x, mod_l, 0, False, [w_main, w_gate], [BF16, F32], "even_proj_lat")
            yc, yt = _even_call(pc, pt, gc, gt, conv_h, hp, e_norm_w[i], ret_cos, ret_sin)
        else:
            lam_init = 0.8 - 0.6 * math.exp(-0.3 * li)
            w_qkv = o_w_qkv[i].astype(BF16)
            (pc,) = _proj_call(ctx, mod_l, 0, True, [w_qkv], [BF16], "odd_proj_ctx")
            (pt,) = _proj_call(x, mod_l, 0, False, [w_qkv], [BF16], "odd_proj_lat")
            yt = _attn_call(pt, pc, diff_tabs, o_lambda[i], o_subln_w[i], lam_init)
            w_o = o_w_out[i].astype(BF16)
            yc = None
            if not last:
                raise NotImplementedError("context update after a differential-attention layer")
        wg, wu, wd = f_w_gate[li].astype(BF16), f_w_up[li].astype(BF16), f_w_down[li].astype(BF16)
        wcv = f_conv[li].reshape(9, -1)
        x1, hf = _outln_call(yt, x, mod_l, 2, False, w_o, ln_g[li, 0], ln_b[li, 0], alpha, 3, "mix_out_lat")
        yf = _ffn_call(hf, wg, wu, wcv, wd, GRID_W, "ffn_lat")
        x = _outln_call(yf, x1, mod_l, 5, False, None, ln_g[li, 1], ln_b[li, 1], alpha, None, "ffn_ln_lat")
        if not last:
            c1, hcf = _outln_call(yc, ctx, mod_l, 2, True, w_o, ln_g[li, 0], ln_b[li, 0], alpha, 3, "mix_out_ctx")
            ycf = _ffn_call(hcf, wg, wu, wcv, wd, lc, "ffn_ctx")
            ctx = _outln_call(ycf, c1, mod_l, 5, True, None, ln_g[li, 1], ln_b[li, 1], alpha, None, "ffn_ln_ctx")
    return x
```

```python
import functools
import math

import jax
import jax.numpy as jnp
from jax import lax
from jax.experimental import pallas as pl
from jax.experimental.pallas import tpu as pltpu

F32 = jnp.float32
BF16 = jnp.bfloat16

GRID_W = 64
DN_HEADS = 4
RET_HEADS = 4
HEAD_DIM = 128
DIFF_HEADS = 8
DIFF_HD = 64
DIFF_DV = 2 * DIFF_HD
ROPE_BASE = 10000.0
LN_EPS = 1e-5
NORM_EPS = 1e-6

CHUNK = 128
INV_BASE = 16
PAR_CHUNKS = 4
ATTN_TQ = 1024
ATTN_SUB = 128
SCORE_AHEAD = 3
LOG2_E = 1.4426950408889634
GATE_AHEAD = 2
FFN_ROWS = 512
FFN_SUB = 64
BF16_ROWS = 16
VMEM_LIMIT = 56 * 1024 * 1024


def _dot(a, b):
    return jnp.dot(a, b, preferred_element_type=F32)


def _silu(x):
    return x * jax.nn.sigmoid(x)


def _softplus(x):
    return jnp.maximum(x, 0.0) + jnp.log1p(jnp.exp(-jnp.abs(x)))


def _cparams(sem):
    return pltpu.CompilerParams(dimension_semantics=sem, vmem_limit_bytes=VMEM_LIMIT)


def _mod_kernel(cc_ref, w_ref, b_ref, o_ref):
    a = _silu(cc_ref[...])
    w = w_ref[0]
    a_hi = a.astype(BF16)
    a_lo = (a - a_hi.astype(F32)).astype(BF16)
    w_hi = w.astype(BF16)
    w_lo = (w - w_hi.astype(F32)).astype(BF16)
    o_ref[0] = _dot(a_hi, w_hi) + _dot(a_hi, w_lo) + _dot(a_lo, w_hi) + b_ref[0]


def _mod_call(cc, mod_w, mod_b):
    depth, d, n = mod_w.shape
    r = cc.shape[0]
    tn = min(n, 1536)
    return pl.pallas_call(
        _mod_kernel,
        grid=(depth, n // tn),
        in_specs=[pl.BlockSpec((r, d), lambda l, j: (0, 0)),
                  pl.BlockSpec((1, d, tn), lambda l, j: (l, 0, j)),
                  pl.BlockSpec((1, 1, tn), lambda l, j: (l, 0, j))],
        out_specs=pl.BlockSpec((1, r, tn), lambda l, j: (l, 0, j)),
        out_shape=jax.ShapeDtypeStruct((depth, r, n), F32),
        compiler_params=_cparams(("parallel", "parallel")),
        name="adaln_maps",
    )(cc, mod_w, mod_b.reshape(depth, 1, n))


def _mod_spec(d, k, is_ctx, nb, ngrid):
    if ngrid == 2:
        imap = (lambda b, i: (nb, 0, k)) if is_ctx else (lambda b, i: (b, 0, k))
    else:
        imap = (lambda b, i, j: (nb, 0, k)) if is_ctx else (lambda b, i, j: (b, 0, k))
    return pl.BlockSpec((1, 1, d), imap)


def _proj_kernel(x_ref, sh_ref, sc_ref, *refs, n_w, tn):
    w_refs, o_refs = refs[:n_w], refs[n_w:]
    h = (x_ref[0] * (1.0 + sc_ref[0]) + sh_ref[0]).astype(BF16)
    for w_ref, o_ref in zip(w_refs, o_refs):
        n = w_ref.shape[1]
        for n0 in range(0, n, tn):
            n1 = min(n0 + tn, n)
            o_ref[0, :, n0:n1] = _dot(h, w_ref[:, n0:n1]).astype(o_ref.dtype)


def _proj_call(x, mod_l, k_shift, is_ctx, weights, out_dtypes, name):
    b, l, d = x.shape
    nb = mod_l.shape[0] - 1
    tm = min(l, 512)
    in_specs = [pl.BlockSpec((1, tm, d), lambda bi, i: (bi, i, 0)),
                _mod_spec(d, k_shift, is_ctx, nb, 2),
                _mod_spec(d, k_shift + 1, is_ctx, nb, 2)]
    out_specs, out_shape = [], []
    for w, dt in zip(weights, out_dtypes):
        n = w.shape[1]
        in_specs.append(pl.BlockSpec((d, n), lambda bi, i: (0, 0)))
        out_specs.append(pl.BlockSpec((1, tm, n), lambda bi, i: (bi, i, 0)))
        out_shape.append(jax.ShapeDtypeStruct((b, l, n), dt))
    return pl.pallas_call(
        functools.partial(_proj_kernel, n_w=len(weights), tn=512),
        grid=(b, l // tm),
        in_specs=in_specs, out_specs=out_specs, out_shape=out_shape,
        compiler_params=_cparams(("parallel", "parallel")),
        name=name,
    )(x, mod_l, mod_l, *weights)


def _outln_kernel(*refs, alpha, has_w, has_h):
    refs = list(refs)
    y_ref, x_ref, g_ref = refs[:3]
    refs = refs[3:]
    w_ref = refs.pop(0) if has_w else None
    lng_ref, lnb_ref = refs[:2]
    refs = refs[2:]
    if has_h:
        sh_ref, sc_ref, o_ref, h_ref = refs
    else:
        (o_ref,) = refs
    t = _dot(y_ref[0], w_ref[...]) if has_w else y_ref[0].astype(F32)
    z = alpha * x_ref[0] + g_ref[0] * t
    mu = jnp.mean(z, axis=-1, keepdims=True)
    zc = z - mu
    var = jnp.mean(zc * zc, axis=-1, keepdims=True)
    o = zc * lax.rsqrt(var + LN_EPS) * lng_ref[...] + lnb_ref[...]
    o_ref[0] = o
    if has_h:
        h_ref[0] = (o * (1.0 + sc_ref[0]) + sh_ref[0]).astype(BF16)


def _outln_call(y, x, mod_l, k_gate, is_ctx, w, ln_g, ln_b, alpha, k_shift_next, name):
    b, l, d = x.shape
    kdim = y.shape[-1]
    nb = mod_l.shape[0] - 1
    tm = min(l, 512)
    has_w, has_h = w is not None, k_shift_next is not None
    args = [y, x, mod_l]
    in_specs = [pl.BlockSpec((1, tm, kdim), lambda bi, i: (bi, i, 0)),
                pl.BlockSpec((1, tm, d), lambda bi, i: (bi, i, 0)),
                _mod_spec(d, k_gate, is_ctx, nb, 2)]
    if has_w:
        args.append(w)
        in_specs.append(pl.BlockSpec((kdim, d), lambda bi, i: (0, 0)))
    args += [ln_g.reshape(1, d), ln_b.reshape(1, d)]
    in_specs += [pl.BlockSpec((1, d), lambda bi, i: (0, 0))] * 2
    out_specs = [pl.BlockSpec((1, tm, d), lambda bi, i: (bi, i, 0))]
    out_shape = [jax.ShapeDtypeStruct((b, l, d), F32)]
    if has_h:
        args += [mod_l, mod_l]
        in_specs += [_mod_spec(d, k_shift_next, is_ctx, nb, 2), _mod_spec(d, k_shift_next + 1, is_ctx, nb, 2)]
        out_specs.append(pl.BlockSpec((1, tm, d), lambda bi, i: (bi, i, 0)))
        out_shape.append(jax.ShapeDtypeStruct((b, l, d), BF16))
    res = pl.pallas_call(
        functools.partial(_outln_kernel, alpha=alpha, has_w=has_w, has_h=has_h),
        grid=(b, l // tm),
        in_specs=in_specs, out_specs=out_specs, out_shape=out_shape,
        compiler_params=_cparams(("parallel", "parallel")),
        name=name,
    )(*args)
    return res if has_h else res[0]


def _ffn_kernel(h_ref, wg_ref, wu_ref, wc_ref, wd_ref, ml_ref, mr_ref, o_ref, apad, acc, *, l, w, pad, rb):
    f = pl.program_id(1)
    tf = wg_ref.shape[1]

    @pl.when(f == 0)
    def _():
        acc[...] = jnp.zeros_like(acc)
        apad[0:pad, :] = jnp.zeros((pad, tf), F32)
        apad[pad + l:pad + l + pad, :] = jnp.zeros((pad, tf), F32)

    nblk = l // rb
    dis = (-1, 0, 1) if l // w > 1 else (0,)
    wrow = [wc_ref[k:k + 1, :] for k in range(9)]

    def gate(i):
        apad[pad + i * rb:pad + (i + 1) * rb, :] = _dot(h_ref[0, i * rb:(i + 1) * rb, :], wg_ref[...])

    def colsum(dj, lo, n):
        tot = None
        for di in dis:
            term = apad[lo + di * w:lo + di * w + n, :] * wrow[(di + 1) * 3 + dj + 1]
            tot = term if tot is None else tot + term
        return tot

    def down(i, t):
        acc[i * rb:(i + 1) * rb, :] += _dot(t, wd_ref[...])

    for i in range(min(GATE_AHEAD, nblk)):
        gate(i)
    t_prev = None
    for i in range(nblk):
        if i + GATE_AHEAD < nblk:
            gate(i + GATE_AHEAD)
        r0 = i * rb
        base = pad + r0
        u = _dot(h_ref[0, r0:r0 + rb, :], wu_ref[...])
        if t_prev is not None:
            down(i - 1, t_prev)
        ts = []
        for s0 in range(0, rb, FFN_SUB):
            lo = base + s0
            conv = (colsum(0, lo, FFN_SUB)
                    + colsum(-1, lo - 8, FFN_SUB + 16)[7:7 + FFN_SUB, :] * ml_ref[r0 + s0:r0 + s0 + FFN_SUB, :]
                    + colsum(1, lo - 8, FFN_SUB + 16)[9:9 + FFN_SUB, :] * mr_ref[r0 + s0:r0 + s0 + FFN_SUB, :])
            ts.append((_silu(conv) * u[s0:s0 + FFN_SUB, :]).astype(BF16))
        t_prev = jnp.concatenate(ts, axis=0)
    down(nblk - 1, t_prev)

    @pl.when(f == pl.num_programs(1) - 1)
    def _():
        o_ref[0] = acc[...].astype(o_ref.dtype)


def _ffn_call(hf, w_gate, w_up, w_conv, w_down, grid_w, name):
    b, l, d = hf.shape
    dff = w_gate.shape[1]
    tf = 256 if dff % 256 == 0 else 128
    rb = min(l, FFN_ROWS)
    pad = grid_w + 8 if l // grid_w > 1 else 8
    col = jnp.arange(l) % grid_w
    ones = jnp.ones((l, tf), F32)
    m_left = ones * (col != 0)[:, None]
    m_right = ones * (col != grid_w - 1)[:, None]
    return pl.pallas_call(
        functools.partial(_ffn_kernel, l=l, w=grid_w, pad=pad, rb=rb),
        grid=(b, dff // tf),
        in_specs=[pl.BlockSpec((1, l, d), lambda bi, f: (bi, 0, 0)),
                  pl.BlockSpec((d, tf), lambda bi, f: (0, f)),
                  pl.BlockSpec((d, tf), lambda bi, f: (0, f)),
                  pl.BlockSpec((9, tf), lambda bi, f: (0, f)),
                  pl.BlockSpec((tf, d), lambda bi, f: (f, 0)),
                  pl.BlockSpec((l, tf), lambda bi, f: (0, 0)),
                  pl.BlockSpec((l, tf), lambda bi, f: (0, 0))],
        out_specs=pl.BlockSpec((1, l, d), lambda bi, f: (bi, 0, 0)),
        out_shape=jax.ShapeDtypeStruct((b, l, d), BF16),
        scratch_shapes=[pltpu.VMEM((l + 2 * pad, tf), F32), pltpu.VMEM((l, d), F32)],
        compiler_params=_cparams(("parallel", "arbitrary")),
        name=name,
    )(hf, w_gate, w_up, w_conv, w_down, m_left, m_right)


def _even_kernel(pc_ref, pt_ref, gc_ref, gt_ref, cw_ref, hp_ref, nw_ref, cos_ref, sin_ref,
                 yc_ref, yt_ref,
                 u_s, wq_s, qkk_s, cd_s, ri_s, kvf_s, kvb_s, rqf_s, rqb_s, of_s, ob_s, rb_s, *, nc, nt):
    c = CHUNK
    hd = HEAD_DIM
    ntot = nc + nt
    ri = lax.broadcasted_iota(jnp.int32, (c, c), 0)
    ci = lax.broadcasted_iota(jnp.int32, (c, c), 1)
    rif = ri.astype(F32)
    cif = ci.astype(F32)
    eye = (ri == ci).astype(F32)

    def same_block(size):
        sh = size.bit_length() - 1
        return jnp.right_shift(ri, sh) == jnp.right_shift(ci, sh)

    diag_mask = same_block(INV_BASE).astype(F32)
    off_masks = []
    size = INV_BASE
    while size < c:
        off_masks.append(jnp.where(same_block(2 * size), 1.0, 0.0) - jnp.where(same_block(size), 1.0, 0.0))
        size *= 2
    hp = hp_ref[0]
    a_log8, dt8 = hp[0:8, :], hp[8:16, :]
    lg_f = -jnp.exp(hp[16:17, :])
    lg_b = -jnp.exp(hp[17:18, :])
    dec_bi = (jnp.where(ri >= ci, jnp.exp(lg_f * (rif - cif)), 0.0)
              + jnp.where(ci >= ri, jnp.exp(lg_b * (cif - rif)), 0.0))
    gq_f = jnp.exp(lg_f * (rif + 1.0))
    gq_b = jnp.exp(lg_b * (c - rif))
    gk_f = jnp.exp(lg_f * (c - 1.0 - cif))
    gk_b = jnp.exp(lg_b * cif)
    cd_f = jnp.exp(lg_f * c)
    cd_b = jnp.exp(lg_b * c)
    row8 = lax.broadcasted_iota(jnp.int32, (8, c), 0)
    lane8 = lax.broadcasted_iota(jnp.int32, (8, c), 1)

    def gate_rows(g):
        gt8 = g.T[0:8, :]
        la = -jnp.exp(a_log8) * _softplus(gt8 + dt8)
        pre, suf = la, la
        s = 1
        while s < c:
            pre = pre + jnp.where(lane8 >= s, pltpu.roll(pre, s, 1), 0.0)
            suf = suf + jnp.where(lane8 < c - s, pltpu.roll(suf, c - s, 1), 0.0)
            s *= 2
        cum = jnp.where(row8 == 0, pre, suf)
        ecum = pltpu.roll(jnp.exp(cum), 4, 0)
        return jnp.where(row8 < 2, cum, jnp.where(row8 < 4, jax.nn.sigmoid(gt8), jnp.where(row8 < 6, ecum, 0.0)))

    def chunk_prep(src_ref, gsrc_ref, m, ns, off, rot):
        ln = ns * c
        r0 = pl.multiple_of(m * c, c)
        rows = pl.ds(r0, c)
        x = src_ref[0, rows, 0:3 * hd].astype(F32)
        pstart = pl.multiple_of(jnp.maximum(r0 - BF16_ROWS, 0), BF16_ROWS)
        nstart = pl.multiple_of(jnp.minimum(r0 + c, ln - BF16_ROWS), BF16_ROWS)
        prev = src_ref[0, pl.ds(pstart, BF16_ROWS), 0:3 * hd].astype(F32)[BF16_ROWS - 1:BF16_ROWS, :]
        nxt = src_ref[0, pl.ds(nstart, BF16_ROWS), 0:3 * hd].astype(F32)[0:1, :]
        prev = prev * jnp.where(m > 0, 1.0, 0.0)
        nxt = nxt * jnp.where(m < ns - 1, 1.0, 0.0)
        rr = lax.broadcasted_iota(jnp.int32, (c, 3 * hd), 0)
        xm1 = jnp.where(rr == 0, prev, pltpu.roll(x, 1, 0))
        xp1 = jnp.where(rr == c - 1, nxt, pltpu.roll(x, c - 1, 0))
        cw = cw_ref[0]
        y = _silu(xm1 * cw[0:1, :] + x * cw[1:2, :] + xp1 * cw[2:3, :])
        q = y[:, 0:hd]
        k = y[:, hd:2 * hd]
        q = q * (lax.rsqrt(jnp.sum(q * q, axis=-1, keepdims=True) + NORM_EPS) * (hd ** -0.5))
        k = k * lax.rsqrt(jnp.sum(k * k, axis=-1, keepdims=True) + NORM_EPS)
        xt8 = gate_rows(gsrc_ref[0, rows, :])
        xg = jnp.concatenate([xt8, jnp.zeros((c - 8, c), F32)], axis=0).T
        rq = src_ref[0, rows, 4 * hd:5 * hd].astype(F32)
        rk = src_ref[0, rows, 5 * hd:6 * hd].astype(F32)
        if rot:
            cs, sn = cos_ref[rows, :], sin_ref[rows, :]
            rq = rq * cs + pltpu.roll(rq, hd // 2, 1) * sn
            rk = rk * cs + pltpu.roll(rk, hd // 2, 1) * sn
        return dict(n=m + off, q=q, k=k, kt=k.T, v=y[:, 2 * hd:3 * hd], xt8=xt8, xg=xg,
                    rq=rq, rkt=(rk * (hd ** -0.5)).T, rv=src_ref[0, rows, 6 * hd:7 * hd])

    def chunks_work(preps):
        for pr in preps:
            pr["kt16"] = pr["kt"].astype(BF16)
        kks = [_dot(pr["k"].astype(BF16), pr["kt16"]) for pr in preps]
        qks = [_dot(pr["q"].astype(BF16), pr["kt16"]) for pr in preps]
        scs = [_dot(pr["rq"].astype(BF16), pr["rkt"].astype(BF16)) for pr in preps]
        chains = []
        for pr, kk, qk in zip(preps, kks, qks):
            for d in range(2):
                xg, xt8 = pr["xg"], pr["xt8"]
                g, beta, eg = xg[:, d:d + 1], xg[:, 2 + d:3 + d], xg[:, 4 + d:5 + d]
                g_row = xt8[d:d + 1, :]
                last = 0 if d else c - 1
                g_last = xt8[d:d + 1, last:last + 1]
                e = jnp.exp(jnp.minimum(g - g_row, 0.0))
                incl = (ri <= ci) if d else (ri >= ci)
                strict = (ri < ci) if d else (ri > ci)
                nm = -(beta * kk) * jnp.where(strict, e, 0.0)
                nd = nm * diag_mask
                chains.append(dict(pr=pr, d=d, nm=nm, nd16=nd.astype(BF16), p=eye + nd, beta=beta, eg=eg,
                                   qkm=qk * jnp.where(incl, e, 0.0), kgt=pr["kt"] * jnp.exp(g_last - g_row),
                                   cd=jnp.broadcast_to(jnp.exp(g_last), (8, c))))
        for ch in chains:
            ch["npow"] = _dot(ch["nd16"], ch["nd16"])
        lvl = 2
        while lvl * 2 < INV_BASE:
            rs = [_dot(jnp.concatenate([ch["p"], ch["npow"]], axis=0).astype(BF16), ch["npow"].astype(BF16))
                  for ch in chains]
            for ch, r in zip(chains, rs):
                ch["p"] = ch["p"] + r[0:c, :]
                ch["npow"] = r[c:, :]
            lvl *= 2
        rs = [_dot(ch["p"].astype(BF16), ch["npow"].astype(BF16)) for ch in chains]
        for ch, r in zip(chains, rs):
            ch["p"] = ch["p"] + r
        for om in off_masks:
            xs = [_dot(ch["p"].astype(BF16), (ch["nm"] * om).astype(BF16)) for ch in chains]
            rs = [_dot(x.astype(BF16), ch["p"].astype(BF16)) for ch, x in zip(chains, xs)]
            for ch, r in zip(chains, rs):
                ch["p"] = ch["p"] + r
        sols = [_dot(ch["p"].astype(BF16),
                     jnp.concatenate([ch["pr"]["v"] * ch["beta"], ch["pr"]["k"] * (ch["beta"] * ch["eg"])],
                                     axis=1).astype(BF16)) for ch in chains]
        rets = [_dot(jnp.concatenate([sc * dec_bi, pr["rkt"] * gk_f, pr["rkt"] * gk_b], axis=0).astype(BF16), pr["rv"])
                for pr, sc in zip(preps, scs)]
        for ch, sol in zip(chains, sols):
            idx = ch["d"] * ntot + ch["pr"]["n"]
            u_s[idx] = sol[:, 0:hd]
            wq_s[idx] = jnp.concatenate([sol[:, hd:2 * hd], ch["pr"]["q"] * ch["eg"]], axis=0).astype(BF16)
            qkk_s[idx] = jnp.concatenate([ch["qkm"], ch["kgt"]], axis=0).astype(BF16)
            cd_s[idx] = ch["cd"]
        for pr, r in zip(preps, rets):
            n = pr["n"]
            ri_s[n] = r[0:c, :]
            kvf_s[n] = r[c:2 * c, :]
            kvb_s[n] = r[2 * c:3 * c, :]
            rqf_s[n] = (pr["rq"] * gq_f).astype(BF16)
            rqb_s[n] = (pr["rq"] * gq_b).astype(BF16)

    def segment(src_ref, gsrc_ref, ns, off, rot):
        par = math.gcd(ns, PAR_CHUNKS)

        def body(j, carry):
            chunks_work([chunk_prep(src_ref, gsrc_ref, j * par + t, ns, off, rot) for t in range(par)])
            return carry

        lax.fori_loop(0, ns // par, body, 0)

    segment(pc_ref, gc_ref, nc, 0, False)
    segment(pt_ref, gt_ref, nt, nc, True)

    def emit(src_ref, dst_ref, m, o, r):
        rows = pl.ds(pl.multiple_of(m * c, c), c)
        zg = src_ref[0, rows, 3 * hd:4 * hd].astype(F32)
        dn = o * lax.rsqrt(jnp.mean(o * o, axis=-1, keepdims=True) + NORM_EPS) * nw_ref[...] * _silu(zg)
        mu = jnp.mean(r, axis=-1, keepdims=True)
        rc = r - mu
        rg = src_ref[0, rows, 7 * hd:8 * hd].astype(F32)
        rt = rc * lax.rsqrt(jnp.mean(rc * rc, axis=-1, keepdims=True) + NORM_EPS) * _silu(rg)
        dst_ref[0, rows, 0:hd] = dn.astype(dst_ref.dtype)
        dst_ref[0, rows, hd:2 * hd] = rt.astype(dst_ref.dtype)

    def step(i, carry, tail):
        s_f, s_b, t_f, t_b = carry
        nf = i
        nbk = jnp.where(i < nc, nc - 1 - i, ntot - 1 - (i - nc))
        idxs = (nf, ntot + nbk)
        r1s = [_dot(wq_s[idx], st.astype(BF16)) for idx, st in zip(idxs, (s_f, s_b))]
        r_f = ri_s[nf] + _dot(rqf_s[nf], t_f.astype(BF16))
        r_b = _dot(rqb_s[nbk], t_b.astype(BF16))
        r2s = [_dot(qkk_s[idx], (u_s[idx] - r1[0:c, :]).astype(BF16)) for idx, r1 in zip(idxs, r1s)]
        o_f, o_b = [r1[c:, :] + r2[0:c, :] for r1, r2 in zip(r1s, r2s)]
        s_f, s_b = [cd_s[idx][0:1, :] * st + r2[c:, :] for idx, st, r2 in zip(idxs, (s_f, s_b), r2s)]
        t_f = cd_f * t_f + kvf_s[nf]
        t_b = cd_b * t_b + kvb_s[nbk]
        if tail:
            other_b, other_rb, other_f, other_rf = ob_s[nf], rb_s[nf], of_s[nbk], ri_s[nbk]
            emit(pt_ref, yt_ref, nf - nc, o_f + other_b, r_f + other_rb)
            emit(pt_ref, yt_ref, nbk - nc, other_f + o_b, other_rf + r_b)
        else:
            of_s[nf] = o_f
            ob_s[nbk] = o_b
            ri_s[nf] = r_f
            rb_s[nbk] = r_b
        return s_f, s_b, t_f, t_b

    n_head = nc + nt // 2
    z = jnp.zeros((hd, hd), F32)
    carry = lax.fori_loop(0, n_head, functools.partial(step, tail=False), (z, z, z, z))
    lax.fori_loop(n_head, ntot, functools.partial(step, tail=True), carry)

    def ctx_finish(m, carry):
        emit(pc_ref, yc_ref, m, of_s[m] + ob_s[m], ri_s[m] + rb_s[m])
        return carry

    lax.fori_loop(0, nc, ctx_finish, 0)


def _even_call(pc, pt, gc, gt, conv_h, hp, norm_w, cos_t, sin_t):
    b, lc, _ = pc.shape
    lt = pt.shape[1]
    nc, nt = lc // CHUNK, lt // CHUNK
    assert lc % CHUNK == 0 and lt % (2 * CHUNK) == 0
    ntot = nc + nt
    nh = DN_HEADS
    hd = HEAD_DIM
    blk = 8 * hd
    f32_scr = pltpu.VMEM((ntot, CHUNK, CHUNK), F32)
    bf16_scr = pltpu.VMEM((ntot, CHUNK, CHUNK), BF16)
    return pl.pallas_call(
        functools.partial(_even_kernel, nc=nc, nt=nt),
        grid=(b, nh),
        in_specs=[pl.BlockSpec((1, lc, blk), lambda bi, h: (bi, 0, h)),
                  pl.BlockSpec((1, lt, blk), lambda bi, h: (bi, 0, h)),
                  pl.BlockSpec((1, lc, hd), lambda bi, h: (bi, 0, h)),
                  pl.BlockSpec((1, lt, hd), lambda bi, h: (bi, 0, h)),
                  pl.BlockSpec((1, 3, 3 * hd), lambda bi, h: (h, 0, 0)),
                  pl.BlockSpec((1, 24, hd), lambda bi, h: (h, 0, 0)),
                  pl.BlockSpec((1, hd), lambda bi, h: (0, 0)),
                  pl.BlockSpec((lt, hd), lambda bi, h: (0, 0)),
                  pl.BlockSpec((lt, hd), lambda bi, h: (0, 0))],
        out_specs=[pl.BlockSpec((1, lc, 2 * hd), lambda bi, h: (bi, 0, h)),
                   pl.BlockSpec((1, lt, 2 * hd), lambda bi, h: (bi, 0, h))],
        out_shape=[jax.ShapeDtypeStruct((b, lc, nh * 2 * hd), BF16),
                   jax.ShapeDtypeStruct((b, lt, nh * 2 * hd), BF16)],
        scratch_shapes=[pltpu.VMEM((2 * ntot, CHUNK, CHUNK), F32),
                        pltpu.VMEM((2 * ntot, 2 * CHUNK, CHUNK), BF16),
                        pltpu.VMEM((2 * ntot, 2 * CHUNK, CHUNK), BF16),
                        pltpu.VMEM((2 * ntot, 8, CHUNK), F32),
                        f32_scr, f32_scr, f32_scr,
                        bf16_scr, bf16_scr,
                        f32_scr, f32_scr, f32_scr],
        compiler_params=_cparams(("parallel", "arbitrary")),
        name="deltanet_retention",
    )(pc, pt, gc, gt, conv_h, hp, norm_w.reshape(1, hd), cos_t, sin_t)


def _rot2d(x, c_ref, s1_ref, s2_ref):
    return x * c_ref[...] + pltpu.roll(x, 96, 1) * s1_ref[...] + pltpu.roll(x, 32, 1) * s2_ref[...]


def _attn_kernel(q_ref, kt_ref, vt_ref, kc_ref, vc_ref, qc_ref, qs1_ref, qs2_ref, kc_t, ks1_t, ks2_t,
                 lam_ref, sw_ref, o_ref, kfull, vfull, *, lt, lc, lam_init):
    qi = pl.program_id(2)
    hd = DIFF_HD
    lk = lt + lc

    @pl.when(qi == 0)
    def _():
        for r0 in range(0, lt, 128):
            kr = _rot2d(kt_ref[0, r0:r0 + 128, :].astype(F32), kc_t.at[r0:r0 + 128, :],
                        ks1_t.at[r0:r0 + 128, :], ks2_t.at[r0:r0 + 128, :]).T
            kfull[0, :, r0:r0 + 128] = kr[0:hd, :].astype(BF16)
            kfull[1, :, r0:r0 + 128] = kr[hd:2 * hd, :].astype(BF16)
        for r0 in range(0, lc, 128):
            kr = kc_ref[0, r0:r0 + 128, :].astype(F32).T
            kfull[0, :, lt + r0:lt + r0 + 128] = kr[0:hd, :].astype(BF16)
            kfull[1, :, lt + r0:lt + r0 + 128] = kr[hd:2 * hd, :].astype(BF16)
        lane = lax.broadcasted_iota(jnp.int32, (lk, 128), 1)
        vfull[0:lt, 0:128] = vt_ref[0].astype(vfull.dtype)
        vfull[lt:lk, 0:128] = vc_ref[0].astype(vfull.dtype)
        vfull[:, 128:256] = jnp.where(lane == 0, 1.0, 0.0).astype(BF16)

    lp = lam_ref[...]
    lam = (jnp.exp(jnp.sum(lp[0:1, :] * lp[1:2, :], axis=-1, keepdims=True))
           - jnp.exp(jnp.sum(lp[2:3, :] * lp[3:4, :], axis=-1, keepdims=True)) + lam_init)
    q = (_rot2d(q_ref[0].astype(F32), qc_ref, qs1_ref, qs2_ref) * (hd ** -0.5 * LOG2_E)).astype(BF16)
    tq = q.shape[0]
    units = [(r0, comp) for r0 in range(0, tq, ATTN_SUB) for comp in range(2)]

    def score(r0, comp):
        return _dot(q[r0:r0 + ATTN_SUB, comp * hd:(comp + 1) * hd], kfull[comp])

    def weighted(s):
        p = jnp.exp2(s - jnp.max(s, axis=-1, keepdims=True)).astype(BF16)
        r = _dot(p, vfull[...])
        return r[:, 0:128] / r[:, 128:129]

    scores = [score(*u) for u in units[:SCORE_AHEAD]]
    outs = []
    for i in range(len(units)):
        if i + SCORE_AHEAD < len(units):
            scores.append(score(*units[i + SCORE_AHEAD]))
        outs.append(weighted(scores[i]))
        scores[i] = None
    for j, r0 in enumerate(range(0, tq, ATTN_SUB)):
        o = outs[2 * j] - lam * outs[2 * j + 1]
        o = o * lax.rsqrt(jnp.mean(o * o, axis=-1, keepdims=True) + NORM_EPS) * sw_ref[...] * (1.0 - lam_init)
        o_ref[0, r0:r0 + ATTN_SUB, :] = o.astype(o_ref.dtype)


def _attn_call(pt, pc, tabs, lam_p, subln_w, lam_init):
    b, lt, _ = pt.shape
    lc = pc.shape[1]
    nh = DIFF_HEADS
    tq = min(lt, ATTN_TQ)
    ct, s1t, s2t = tabs
    tab_q = pl.BlockSpec((tq, 128), lambda bi, h, qi: (qi, 0))
    tab_k = pl.BlockSpec((lt, 128), lambda bi, h, qi: (0, 0))
    return pl.pallas_call(
        functools.partial(_attn_kernel, lt=lt, lc=lc, lam_init=lam_init),
        grid=(b, nh, lt // tq),
        in_specs=[pl.BlockSpec((1, tq, 128), lambda bi, h, qi: (bi, qi, h)),
                  pl.BlockSpec((1, lt, 128), lambda bi, h, qi: (bi, 0, nh + h)),
                  pl.BlockSpec((1, lt, 128), lambda bi, h, qi: (bi, 0, 2 * nh + h)),
                  pl.BlockSpec((1, lc, 128), lambda bi, h, qi: (bi, 0, nh + h)),
                  pl.BlockSpec((1, lc, 128), lambda bi, h, qi: (bi, 0, 2 * nh + h)),
                  tab_q, tab_q, tab_q, tab_k, tab_k, tab_k,
                  pl.BlockSpec((4, DIFF_HD), lambda bi, h, qi: (0, 0)),
                  pl.BlockSpec((1, 128), lambda bi, h, qi: (0, 0))],
        out_specs=pl.BlockSpec((1, tq, 128), lambda bi, h, qi: (bi, qi, h)),
        out_shape=jax.ShapeDtypeStruct((b, lt, nh * DIFF_DV), BF16),
        scratch_shapes=[pltpu.VMEM((2, DIFF_HD, lt + lc), BF16), pltpu.VMEM((lt + lc, 256), BF16)],
        compiler_params=_cparams(("parallel", "parallel", "arbitrary")),
        name="diff_attention",
    )(pt, pt, pt, pc, pc, ct, s1t, s2t, ct, s1t, s2t, lam_p, subln_w.reshape(1, 128))


def _even_weights(w_in, conv_w, a_log, dt_bias, ret_decay, w_out):
    nh, hd = DN_HEADS, HEAD_DIM
    o_z, o_a, o_b = 3 * nh * hd, 4 * nh * hd, 4 * nh * hd + 2 * nh
    o_rq = o_b + 2 * nh

    def col(base, h):
        return w_in[:, base + h * hd:base + (h + 1) * hd]

    main, gate, conv_h, hp = [], [], [], []
    d = w_in.shape[0]
    for h in range(nh):
        main += [col(0, h), col(nh * hd, h), col(2 * nh * hd, h), col(o_z, h),
                 col(o_rq, h), col(o_rq + nh * hd, h), col(o_rq + 2 * nh * hd, h), col(o_rq + 3 * nh * hd, h)]
        gate += [w_in[:, o_a + h:o_a + h + 1], w_in[:, o_a + nh + h:o_a + nh + h + 1],
                 w_in[:, o_b + h:o_b + h + 1], w_in[:, o_b + nh + h:o_b + nh + h + 1],
                 jnp.zeros((d, hd - 4), w_in.dtype)]
        conv_h.append(jnp.concatenate([conv_w[:, h * hd:(h + 1) * hd], conv_w[:, (nh + h) * hd:(nh + h + 1) * hd],
                                       conv_w[:, (2 * nh + h) * hd:(2 * nh + h + 1) * hd]], axis=1))
        ones = jnp.ones((hd,), F32)
        zero = jnp.zeros((hd,), F32)
        hp.append(jnp.stack([ones * a_log[0, h], ones * a_log[1, h]] + [zero] * 6
                            + [ones * dt_bias[0, h], ones * dt_bias[1, h]] + [zero] * 6
                            + [ones * ret_decay[0, h], ones * ret_decay[1, h]] + [zero] * 6))
    w_main = jnp.concatenate(main, axis=1).astype(BF16)
    w_gate = jnp.concatenate(gate, axis=1).astype(BF16)
    w_o = jnp.concatenate([w_out[(j * nh + h) * hd:(j * nh + h + 1) * hd] for h in range(nh) for j in range(2)],
                          axis=0).astype(BF16)
    return w_main, w_gate, jnp.stack(conv_h).astype(F32), jnp.stack(hp).astype(F32), w_o


def _ret_tables(l):
    half = HEAD_DIM // 2
    inv = ROPE_BASE ** (-jnp.arange(half, dtype=F32) / half)
    ang = jnp.arange(l).astype(F32)[:, None] * inv[None]
    cs, sn = jnp.cos(ang), jnp.sin(ang)
    return jnp.concatenate([cs, cs], -1), jnp.concatenate([-sn, sn], -1)


def _diff_tables(l):
    n = DIFF_HD // 4
    pos = jnp.arange(l)
    inv = ROPE_BASE ** (-jnp.arange(n, dtype=F32) / n)
    ang = jnp.concatenate([(pos // GRID_W).astype(F32)[:, None] * inv[None],
                           (pos % GRID_W).astype(F32)[:, None] * inv[None]], -1)
    cs, sn, zero = jnp.cos(ang), jnp.sin(ang), jnp.zeros_like(ang)
    c64 = jnp.concatenate([cs, cs], -1)
    s1 = jnp.concatenate([-sn, zero], -1)
    s2 = jnp.concatenate([zero, sn], -1)
    return tuple(jnp.concatenate([t, t], -1) for t in (c64, s1, s2))


def kernel(x, c, ctx, c_ctx, mod_w, mod_b, ln_g, ln_b, e_w_in, e_conv, e_a_log, e_dt_bias, e_norm_w, e_ret_decay,
           e_w_out, o_w_qkv, o_lambda, o_subln_w, o_w_out, f_w_gate, f_w_up, f_conv, f_w_down):
    depth = mod_w.shape[0]
    b, l, d = x.shape
    lc = ctx.shape[1]
    alpha = (2 * depth) ** 0.25
    n_rows = -(-(b + 1) // 8) * 8
    cc = jnp.concatenate([c, c_ctx[None, :], jnp.zeros((n_rows - b - 1, d), F32)], axis=0)
    mod = _mod_call(cc, mod_w, mod_b).reshape(depth, n_rows, 1, 6 * d)[:, :b + 1]
    ret_cos, ret_sin = _ret_tables(l)
    diff_tabs = _diff_tables(l)
    for li in range(depth):
        last = li == depth - 1
        i = li // 2
        mod_l = mod[li]
        if li % 2 == 0:
            w_main, w_gate, conv_h, hp, w_o = _even_weights(e_w_in[i], e_conv[i], e_a_log[i], e_dt_bias[i],
                                                            e_ret_decay[i], e_w_out[i])
            pc, gc = _proj_call(ctx, mod_l, 0, True, [w_main, w_gate], [BF16, F32], "even_proj_ctx")
            pt, gt = _proj_call(x, mod_l, 0, False, [w_main, w_gate], [BF16, F32], "even_proj_lat")
            yc, yt = _even_call(pc, pt, gc, gt, conv_h, hp, e_norm_w[i], ret_cos, ret_sin)
        else:
            lam_init = 0.8 - 0.6 * math.exp(-0.3 * li)
            w_qkv = o_w_qkv[i].astype(BF16)
            (pc,) = _proj_call(ctx, mod_l, 0, True, [w_qkv], [BF16], "odd_proj_ctx")
            (pt,) = _proj_call(x, mod_l, 0, False, [w_qkv], [BF16], "odd_proj_lat")
            yt = _attn_call(pt, pc, diff_tabs, o_lambda[i], o_subln_w[i], lam_init)
            w_o = o_w_out[i].astype(BF16)
            yc = None
            if not last:
                raise NotImplementedError("context update after a differential-attention layer")
        wg, wu, wd = f_w_gate[li].astype(BF16), f_w_up[li].astype(BF16), f_w_down[li].astype(BF16)
        wcv = f_conv[li].reshape(9, -1)
        x1, hf = _outln_call(yt, x, mod_l, 2, False, w_o, ln_g[li, 0], ln_b[li, 0], alpha, 3, "mix_out_lat")
        yf = _ffn_call(hf, wg, wu, wcv, wd, GRID_W, "ffn_lat")
        x = _outln_call(yf, x1, mod_l, 5, False, None, ln_g[li, 1], ln_b[li, 1], alpha, None, "ffn_ln_lat")
        if not last:
            c1, hcf = _outln_call(yc, ctx, mod_l, 2, True, w_o, ln_g[li, 0], ln_b[li, 0], alpha, 3, "mix_out_ctx")
            ycf = _ffn_call(hcf, wg, wu, wcv, wd, lc, "ffn_ctx")
            ctx = _outln_call(ycf, c1, mod_l, 5, True, None, ln_g[li, 1], ln_b[li, 1], alpha, None, "ffn_ln_ctx")
    return x
```

```python
import functools
import math

import jax
import jax.numpy as jnp
from jax import lax
from jax.experimental import pallas as pl
from jax.experimental.pallas import tpu as pltpu

F32 = jnp.float32
BF16 = jnp.bfloat16

GRID_W = 64
DN_HEADS = 4
RET_HEADS = 4
HEAD_DIM = 128
DIFF_HEADS = 8
DIFF_HD = 64
DIFF_DV = 2 * DIFF_HD
ROPE_BASE = 10000.0
LN_EPS = 1e-5
NORM_EPS = 1e-6

CHUNK = 128
INV_BASE = 16
PAR_CHUNKS = 4
ATTN_TQ = 1024
ATTN_SUB = 128
SCORE_AHEAD = 3
LOG2_E = 1.4426950408889634
GATE_AHEAD = 2
FFN_ROWS = 512
FFN_LEAD_EDGES = (128, 256)
FFN_TF = 256
FFN_SUB = 64
BF16_ROWS = 16
VMEM_LIMIT = 56 * 1024 * 1024


def _dot(a, b):
    return jnp.dot(a, b, preferred_element_type=F32)


def _silu(x):
    return x * jax.nn.sigmoid(x)


def _softplus(x):
    return jnp.maximum(x, 0.0) + jnp.log1p(jnp.exp(-jnp.abs(x)))


def _cparams(sem):
    return pltpu.CompilerParams(dimension_semantics=sem, vmem_limit_bytes=VMEM_LIMIT)


def _mod_kernel(cc_ref, w_ref, b_ref, o_ref):
    a = _silu(cc_ref[...])
    w = w_ref[0]
    a_hi = a.astype(BF16)
    a_lo = (a - a_hi.astype(F32)).astype(BF16)
    w_hi = w.astype(BF16)
    w_lo = (w - w_hi.astype(F32)).astype(BF16)
    o_ref[0] = _dot(a_hi, w_hi) + _dot(a_hi, w_lo) + _dot(a_lo, w_hi) + b_ref[0]


def _mod_call(cc, mod_w, mod_b):
    depth, d, n = mod_w.shape
    r = cc.shape[0]
    tn = min(n, 1536)
    return pl.pallas_call(
        _mod_kernel,
        grid=(depth, n // tn),
        in_specs=[pl.BlockSpec((r, d), lambda l, j: (0, 0)),
                  pl.BlockSpec((1, d, tn), lambda l, j: (l, 0, j)),
                  pl.BlockSpec((1, 1, tn), lambda l, j: (l, 0, j))],
        out_specs=pl.BlockSpec((1, r, tn), lambda l, j: (l, 0, j)),
        out_shape=jax.ShapeDtypeStruct((depth, r, n), F32),
        compiler_params=_cparams(("parallel", "parallel")),
        name="adaln_maps",
    )(cc, mod_w, mod_b.reshape(depth, 1, n))


def _mod_spec(d, k, is_ctx, nb, ngrid):
    if ngrid == 2:
        imap = (lambda b, i: (nb, 0, k)) if is_ctx else (lambda b, i: (b, 0, k))
    else:
        imap = (lambda b, i, j: (nb, 0, k)) if is_ctx else (lambda b, i, j: (b, 0, k))
    return pl.BlockSpec((1, 1, d), imap)


def _proj_kernel(x_ref, sh_ref, sc_ref, *refs, n_w, tn):
    w_refs, o_refs = refs[:n_w], refs[n_w:]
    h = (x_ref[0] * (1.0 + sc_ref[0]) + sh_ref[0]).astype(BF16)
    for w_ref, o_ref in zip(w_refs, o_refs):
        n = w_ref.shape[1]
        for n0 in range(0, n, tn):
            n1 = min(n0 + tn, n)
            o_ref[0, :, n0:n1] = _dot(h, w_ref[:, n0:n1]).astype(o_ref.dtype)


def _proj_call(x, mod_l, k_shift, is_ctx, weights, out_dtypes, name):
    b, l, d = x.shape
    nb = mod_l.shape[0] - 1
    tm = min(l, 512)
    in_specs = [pl.BlockSpec((1, tm, d), lambda bi, i: (bi, i, 0)),
                _mod_spec(d, k_shift, is_ctx, nb, 2),
                _mod_spec(d, k_shift + 1, is_ctx, nb, 2)]
    out_specs, out_shape = [], []
    for w, dt in zip(weights, out_dtypes):
        n = w.shape[1]
        in_specs.append(pl.BlockSpec((d, n), lambda bi, i: (0, 0)))
        out_specs.append(pl.BlockSpec((1, tm, n), lambda bi, i: (bi, i, 0)))
        out_shape.append(jax.ShapeDtypeStruct((b, l, n), dt))
    return pl.pallas_call(
        functools.partial(_proj_kernel, n_w=len(weights), tn=512),
        grid=(b, l // tm),
        in_specs=in_specs, out_specs=out_specs, out_shape=out_shape,
        compiler_params=_cparams(("parallel", "parallel")),
        name=name,
    )(x, mod_l, mod_l, *weights)


def _outln_kernel(*refs, alpha, has_w, has_h):
    refs = list(refs)
    y_ref, x_ref, g_ref = refs[:3]
    refs = refs[3:]
    w_ref = refs.pop(0) if has_w else None
    lng_ref, lnb_ref = refs[:2]
    refs = refs[2:]
    if has_h:
        sh_ref, sc_ref, o_ref, h_ref = refs
    else:
        (o_ref,) = refs
    t = _dot(y_ref[0], w_ref[...]) if has_w else y_ref[0].astype(F32)
    z = alpha * x_ref[0] + g_ref[0] * t
    mu = jnp.mean(z, axis=-1, keepdims=True)
    zc = z - mu
    var = jnp.mean(zc * zc, axis=-1, keepdims=True)
    o = zc * lax.rsqrt(var + LN_EPS) * lng_ref[...] + lnb_ref[...]
    o_ref[0] = o
    if has_h:
        h_ref[0] = (o * (1.0 + sc_ref[0]) + sh_ref[0]).astype(BF16)


def _outln_call(y, x, mod_l, k_gate, is_ctx, w, ln_g, ln_b, alpha, k_shift_next, name):
    b, l, d = x.shape
    kdim = y.shape[-1]
    nb = mod_l.shape[0] - 1
    tm = min(l, 512)
    has_w, has_h = w is not None, k_shift_next is not None
    args = [y, x, mod_l]
    in_specs = [pl.BlockSpec((1, tm, kdim), lambda bi, i: (bi, i, 0)),
                pl.BlockSpec((1, tm, d), lambda bi, i: (bi, i, 0)),
                _mod_spec(d, k_gate, is_ctx, nb, 2)]
    if has_w:
        args.append(w)
        in_specs.append(pl.BlockSpec((kdim, d), lambda bi, i: (0, 0)))
    args += [ln_g.reshape(1, d), ln_b.reshape(1, d)]
    in_specs += [pl.BlockSpec((1, d), lambda bi, i: (0, 0))] * 2
    out_specs = [pl.BlockSpec((1, tm, d), lambda bi, i: (bi, i, 0))]
    out_shape = [jax.ShapeDtypeStruct((b, l, d), F32)]
    if has_h:
        args += [mod_l, mod_l]
        in_specs += [_mod_spec(d, k_shift_next, is_ctx, nb, 2), _mod_spec(d, k_shift_next + 1, is_ctx, nb, 2)]
        out_specs.append(pl.BlockSpec((1, tm, d), lambda bi, i: (bi, i, 0)))
        out_shape.append(jax.ShapeDtypeStruct((b, l, d), BF16))
    res = pl.pallas_call(
        functools.partial(_outln_kernel, alpha=alpha, has_w=has_w, has_h=has_h),
        grid=(b, l // tm),
        in_specs=in_specs, out_specs=out_specs, out_shape=out_shape,
        compiler_params=_cparams(("parallel", "parallel")),
        name=name,
    )(*args)
    return res if has_h else res[0]


def _ffn_kernel(h_ref, wgu_ref, wc_ref, wd_ref, o_ref, apad, ubuf, acc, *, l, w, pad, rb):
    f = pl.program_id(1)
    tf = wd_ref.shape[0]

    @pl.when(f == 0)
    def _():
        acc[...] = jnp.zeros_like(acc)
        apad[0:pad, :] = jnp.zeros((pad, tf), F32)
        apad[pad + l:pad + l + pad, :] = jnp.zeros((pad, tf), F32)

    edges = [e for e in FFN_LEAD_EDGES if e < rb and e < l] + list(range(rb, l, rb)) + [l] if l > rb else [l]
    blocks = list(zip([0] + edges[:-1], edges))
    nblk = len(blocks)
    dis = (-1, 0, 1) if l // w > 1 else (0,)
    wrow = [wc_ref[k:k + 1, :] for k in range(9)]
    row8 = lax.broadcasted_iota(jnp.int32, (8, tf), 0)

    def gate(i):
        lo, hi = blocks[i]
        gu = _dot(h_ref[0, lo:hi, :], wgu_ref[...])
        apad[pad + lo:pad + hi, :] = gu[:, 0:tf]
        ubuf[lo:hi, :] = gu[:, tf:2 * tf]

    def colsum(dj, lo, n):
        tot = None
        for di in dis:
            term = apad[lo + di * w:lo + di * w + n, :] * wrow[(di + 1) * 3 + dj + 1]
            tot = term if tot is None else tot + term
        return tot

    def down(i, t):
        lo, hi = blocks[i]
        acc[lo:hi, :] += _dot(t, wd_ref[...])

    for i in range(min(GATE_AHEAD, nblk)):
        gate(i)
    t_prev = None
    for i in range(nblk):
        if i + GATE_AHEAD < nblk:
            gate(i + GATE_AHEAD)
        r0, r1 = blocks[i]
        base = pad + r0
        if t_prev is not None:
            down(i - 1, t_prev)
        ts = []
        for s0 in range(0, r1 - r0, FFN_SUB):
            lo = base + s0
            n_ext = FFN_SUB + 16
            left = pltpu.roll(colsum(-1, lo - 8, n_ext), 1, 0)[8:8 + FFN_SUB, :]
            right = pltpu.roll(colsum(1, lo - 8, n_ext), n_ext - 1, 0)[8:8 + FFN_SUB, :]
            if (r0 + s0) % w == 0:
                left = jnp.concatenate([jnp.where(row8 == 0, 0.0, left[0:8, :]), left[8:, :]], axis=0)
            if (r0 + s0 + FFN_SUB) % w == 0:
                right = jnp.concatenate([right[:FFN_SUB - 8, :], jnp.where(row8 == 7, 0.0, right[FFN_SUB - 8:, :])],
                                        axis=0)
            conv = colsum(0, lo, FFN_SUB) + left + right
            ts.append((_silu(conv) * ubuf[r0 + s0:r0 + s0 + FFN_SUB, :]).astype(BF16))
        t_prev = jnp.concatenate(ts, axis=0)
    down(nblk - 1, t_prev)

    @pl.when(f == pl.num_programs(1) - 1)
    def _():
        o_ref[0] = acc[...].astype(o_ref.dtype)


def _ffn_call(hf, w_gate_up, w_conv, w_down, grid_w, name):
    b, l, d = hf.shape
    dff = w_down.shape[0]
    tf = FFN_TF
    rb = min(l, FFN_ROWS)
    pad = grid_w + 8 if l // grid_w > 1 else 8
    assert grid_w % FFN_SUB == 0 and l % rb == 0 and rb % FFN_SUB == 0 and all(e % FFN_SUB == 0 for e in FFN_LEAD_EDGES)
    return pl.pallas_call(
        functools.partial(_ffn_kernel, l=l, w=grid_w, pad=pad, rb=rb),
        grid=(b, dff // tf),
        in_specs=[pl.BlockSpec((1, l, d), lambda bi, f: (bi, 0, 0)),
                  pl.BlockSpec((d, 2 * tf), lambda bi, f: (0, f)),
                  pl.BlockSpec((9, tf), lambda bi, f: (0, f)),
                  pl.BlockSpec((tf, d), lambda bi, f: (f, 0))],
        out_specs=pl.BlockSpec((1, l, d), lambda bi, f: (bi, 0, 0)),
        out_shape=jax.ShapeDtypeStruct((b, l, d), BF16),
        scratch_shapes=[pltpu.VMEM((l + 2 * pad, tf), F32), pltpu.VMEM((l, tf), F32), pltpu.VMEM((l, d), F32)],
        compiler_params=_cparams(("parallel", "arbitrary")),
        name=name,
    )(hf, w_gate_up, w_conv, w_down)


def _even_kernel(pc_ref, pt_ref, gc_ref, gt_ref, cw_ref, hp_ref, nw_ref, cos_ref, sin_ref,
                 yc_ref, yt_ref,
                 u_s, wq_s, qkk_s, cd_s, ri_s, kvf_s, kvb_s, rqf_s, rqb_s, of_s, ob_s, rb_s, *, nc, nt):
    c = CHUNK
    hd = HEAD_DIM
    ntot = nc + nt
    ri = lax.broadcasted_iota(jnp.int32, (c, c), 0)
    ci = lax.broadcasted_iota(jnp.int32, (c, c), 1)
    rif = ri.astype(F32)
    cif = ci.astype(F32)
    eye = (ri == ci).astype(F32)

    def same_block(size):
        sh = size.bit_length() - 1
        return jnp.right_shift(ri, sh) == jnp.right_shift(ci, sh)

    diag_mask = same_block(INV_BASE).astype(F32)
    off_masks = []
    size = INV_BASE
    while size < c:
        off_masks.append(jnp.where(same_block(2 * size), 1.0, 0.0) - jnp.where(same_block(size), 1.0, 0.0))
        size *= 2
    hp = hp_ref[0]
    a_log8, dt8 = hp[0:8, :], hp[8:16, :]
    lg_f = -jnp.exp(hp[16:17, :])
    lg_b = -jnp.exp(hp[17:18, :])
    dec_bi = (jnp.where(ri >= ci, jnp.exp(lg_f * (rif - cif)), 0.0)
              + jnp.where(ci >= ri, jnp.exp(lg_b * (cif - rif)), 0.0))
    gq_f = jnp.exp(lg_f * (rif + 1.0))
    gq_b = jnp.exp(lg_b * (c - rif))
    gk_f = jnp.exp(lg_f * (c - 1.0 - cif))
    gk_b = jnp.exp(lg_b * cif)
    cd_f = jnp.exp(lg_f * c)
    cd_b = jnp.exp(lg_b * c)
    row8 = lax.broadcasted_iota(jnp.int32, (8, c), 0)
    lane8 = lax.broadcasted_iota(jnp.int32, (8, c), 1)

    def gate_rows(g):
        gt8 = g.T[0:8, :]
        la = -jnp.exp(a_log8) * _softplus(gt8 + dt8)
        pre, suf = la, la
        s = 1
        while s < c:
            pre = pre + jnp.where(lane8 >= s, pltpu.roll(pre, s, 1), 0.0)
            suf = suf + jnp.where(lane8 < c - s, pltpu.roll(suf, c - s, 1), 0.0)
            s *= 2
        cum = jnp.where(row8 == 0, pre, suf)
        ecum = pltpu.roll(jnp.exp(cum), 4, 0)
        return jnp.where(row8 < 2, cum, jnp.where(row8 < 4, jax.nn.sigmoid(gt8), jnp.where(row8 < 6, ecum, 0.0)))

    def chunk_prep(src_ref, gsrc_ref, m, ns, off, rot):
        ln = ns * c
        r0 = pl.multiple_of(m * c, c)
        rows = pl.ds(r0, c)
        x = src_ref[0, rows, 0:3 * hd].astype(F32)
        pstart = pl.multiple_of(jnp.maximum(r0 - BF16_ROWS, 0), BF16_ROWS)
        nstart = pl.multiple_of(jnp.minimum(r0 + c, ln - BF16_ROWS), BF16_ROWS)
        prev = src_ref[0, pl.ds(pstart, BF16_ROWS), 0:3 * hd].astype(F32)[BF16_ROWS - 1:BF16_ROWS, :]
        nxt = src_ref[0, pl.ds(nstart, BF16_ROWS), 0:3 * hd].astype(F32)[0:1, :]
        prev = prev * jnp.where(m > 0, 1.0, 0.0)
        nxt = nxt * jnp.where(m < ns - 1, 1.0, 0.0)
        rr = lax.broadcasted_iota(jnp.int32, (c, 3 * hd), 0)
        xm1 = jnp.where(rr == 0, prev, pltpu.roll(x, 1, 0))
        xp1 = jnp.where(rr == c - 1, nxt, pltpu.roll(x, c - 1, 0))
        cw = cw_ref[0]
        y = _silu(xm1 * cw[0:1, :] + x * cw[1:2, :] + xp1 * cw[2:3, :])
        q = y[:, 0:hd]
        k = y[:, hd:2 * hd]
        q = q * (lax.rsqrt(jnp.sum(q * q, axis=-1, keepdims=True) + NORM_EPS) * (hd ** -0.5))
        k = k * lax.rsqrt(jnp.sum(k * k, axis=-1, keepdims=True) + NORM_EPS)
        xt8 = gate_rows(gsrc_ref[0, rows, :])
        xg = jnp.concatenate([xt8, jnp.zeros((c - 8, c), F32)], axis=0).T
        rq = src_ref[0, rows, 4 * hd:5 * hd].astype(F32)
        rk = src_ref[0, rows, 5 * hd:6 * hd].astype(F32)
        if rot:
            cs, sn = cos_ref[rows, :], sin_ref[rows, :]
            rq = rq * cs + pltpu.roll(rq, hd // 2, 1) * sn
            rk = rk * cs + pltpu.roll(rk, hd // 2, 1) * sn
        return dict(n=m + off, q=q, k=k, kt=k.T, v=y[:, 2 * hd:3 * hd], xt8=xt8, xg=xg,
                    rq=rq, rkt=(rk * (hd ** -0.5)).T, rv=src_ref[0, rows, 6 * hd:7 * hd])

    def chunks_work(preps):
        for pr in preps:
            pr["kt16"] = pr["kt"].astype(BF16)
        kks = [_dot(pr["k"].astype(BF16), pr["kt16"]) for pr in preps]
        qks = [_dot(pr["q"].astype(BF16), pr["kt16"]) for pr in preps]
        scs = [_dot(pr["rq"].astype(BF16), pr["rkt"].astype(BF16)) for pr in preps]
        chains = []
        for pr, kk, qk in zip(preps, kks, qks):
            for d in range(2):
                xg, xt8 = pr["xg"], pr["xt8"]
                g, beta, eg = xg[:, d:d + 1], xg[:, 2 + d:3 + d], xg[:, 4 + d:5 + d]
                g_row = xt8[d:d + 1, :]
                last = 0 if d else c - 1
                g_last = xt8[d:d + 1, last:last + 1]
                e = jnp.exp(jnp.minimum(g - g_row, 0.0))
                incl = (ri <= ci) if d else (ri >= ci)
                strict = (ri < ci) if d else (ri > ci)
                nm = -(beta * kk) * jnp.where(strict, e, 0.0)
                nd = nm * diag_mask
                chains.append(dict(pr=pr, d=d, nm=nm, nd16=nd.astype(BF16), p=eye + nd, beta=beta, eg=eg,
                                   qkm=qk * jnp.where(incl, e, 0.0), kgt=pr["kt"] * jnp.exp(g_last - g_row),
                                   cd=jnp.broadcast_to(jnp.exp(g_last), (8, c))))
        for ch in chains:
            ch["npow"] = _dot(ch["nd16"], ch["nd16"])
        lvl = 2
        while lvl * 2 < INV_BASE:
            rs = [_dot(jnp.concatenate([ch["p"], ch["npow"]], axis=0).astype(BF16), ch["npow"].astype(BF16))
                  for ch in chains]
            for ch, r in zip(chains, rs):
                ch["p"] = ch["p"] + r[0:c, :]
                ch["npow"] = r[c:, :]
            lvl *= 2
        rs = [_dot(ch["p"].astype(BF16), ch["npow"].astype(BF16)) for ch in chains]
        for ch, r in zip(chains, rs):
            ch["p"] = ch["p"] + r
        for om in off_masks:
            xs = [_dot(ch["p"].astype(BF16), (ch["nm"] * om).astype(BF16)) for ch in chains]
            rs = [_dot(x.astype(BF16), ch["p"].astype(BF16)) for ch, x in zip(chains, xs)]
            for ch, r in zip(chains, rs):
                ch["p"] = ch["p"] + r
        sols = [_dot(ch["p"].astype(BF16),
                     jnp.concatenate([ch["pr"]["v"] * ch["beta"], ch["pr"]["k"] * (ch["beta"] * ch["eg"])],
                                     axis=1).astype(BF16)) for ch in chains]
        rets = [_dot(jnp.concatenate([sc * dec_bi, pr["rkt"] * gk_f, pr["rkt"] * gk_b], axis=0).astype(BF16), pr["rv"])
                for pr, sc in zip(preps, scs)]
        for ch, sol in zip(chains, sols):
            idx = ch["d"] * ntot + ch["pr"]["n"]
            u_s[idx] = sol[:, 0:hd]
            wq_s[idx] = jnp.concatenate([sol[:, hd:2 * hd], ch["pr"]["q"] * ch["eg"]], axis=0).astype(BF16)
            qkk_s[idx] = jnp.concatenate([ch["qkm"], ch["kgt"]], axis=0).astype(BF16)
            cd_s[idx] = ch["cd"]
        for pr, r in zip(preps, rets):
            n = pr["n"]
            ri_s[n] = r[0:c, :]
            kvf_s[n] = r[c:2 * c, :]
            kvb_s[n] = r[2 * c:3 * c, :]
            rqf_s[n] = (pr["rq"] * gq_f).astype(BF16)
            rqb_s[n] = (pr["rq"] * gq_b).astype(BF16)

    def segment(src_ref, gsrc_ref, ns, off, rot):
        par = math.gcd(ns, PAR_CHUNKS)

        def body(j, carry):
            chunks_work([chunk_prep(src_ref, gsrc_ref, j * par + t, ns, off, rot) for t in range(par)])
            return carry

        lax.fori_loop(0, ns // par, body, 0)

    segment(pc_ref, gc_ref, nc, 0, False)
    segment(pt_ref, gt_ref, nt, nc, True)

    def emit(src_ref, dst_ref, m, o, r):
        rows = pl.ds(pl.multiple_of(m * c, c), c)
        zg = src_ref[0, rows, 3 * hd:4 * hd].astype(F32)
        dn = o * lax.rsqrt(jnp.mean(o * o, axis=-1, keepdims=True) + NORM_EPS) * nw_ref[...] * _silu(zg)
        mu = jnp.mean(r, axis=-1, keepdims=True)
        rc = r - mu
        rg = src_ref[0, rows, 7 * hd:8 * hd].astype(F32)
        rt = rc * lax.rsqrt(jnp.mean(rc * rc, axis=-1, keepdims=True) + NORM_EPS) * _silu(rg)
        dst_ref[0, rows, 0:hd] = dn.astype(dst_ref.dtype)
        dst_ref[0, rows, hd:2 * hd] = rt.astype(dst_ref.dtype)

    def step(i, carry, tail):
        s_f, s_b, t_f, t_b = carry
        nf = i
        nbk = jnp.where(i < nc, nc - 1 - i, ntot - 1 - (i - nc))
        idxs = (nf, ntot + nbk)
        r1s = [_dot(wq_s[idx], st.astype(BF16)) for idx, st in zip(idxs, (s_f, s_b))]
        r_f = ri_s[nf] + _dot(rqf_s[nf], t_f.astype(BF16))
        r_b = _dot(rqb_s[nbk], t_b.astype(BF16))
        r2s = [_dot(qkk_s[idx], (u_s[idx] - r1[0:c, :]).astype(BF16)) for idx, r1 in zip(idxs, r1s)]
        o_f, o_b = [r1[c:, :] + r2[0:c, :] for r1, r2 in zip(r1s, r2s)]
        s_f, s_b = [cd_s[idx][0:1, :] * st + r2[c:, :] for idx, st, r2 in zip(idxs, (s_f, s_b), r2s)]
        t_f = cd_f * t_f + kvf_s[nf]
        t_b = cd_b * t_b + kvb_s[nbk]
        if tail:
            other_b, other_rb, other_f, other_rf = ob_s[nf], rb_s[nf], of_s[nbk], ri_s[nbk]
            emit(pt_ref, yt_ref, nf - nc, o_f + other_b, r_f + other_rb)
            emit(pt_ref, yt_ref, nbk - nc, other_f + o_b, other_rf + r_b)
        else:
            of_s[nf] = o_f
            ob_s[nbk] = o_b
            ri_s[nf] = r_f
            rb_s[nbk] = r_b
        return s_f, s_b, t_f, t_b

    n_head = nc + nt // 2
    z = jnp.zeros((hd, hd), F32)
    carry = lax.fori_loop(0, n_head, functools.partial(step, tail=False), (z, z, z, z))
    lax.fori_loop(n_head, ntot, functools.partial(step, tail=True), carry)

    def ctx_finish(m, carry):
        emit(pc_ref, yc_ref, m, of_s[m] + ob_s[m], ri_s[m] + rb_s[m])
        return carry

    lax.fori_loop(0, nc, ctx_finish, 0)


def _even_call(pc, pt, gc, gt, conv_h, hp, norm_w, cos_t, sin_t):
    b, lc, _ = pc.shape
    lt = pt.shape[1]
    nc, nt = lc // CHUNK, lt // CHUNK
    assert lc % CHUNK == 0 and lt % (2 * CHUNK) == 0
    ntot = nc + nt
    nh = DN_HEADS
    hd = HEAD_DIM
    blk = 8 * hd
    f32_scr = pltpu.VMEM((ntot, CHUNK, CHUNK), F32)
    bf16_scr = pltpu.VMEM((ntot, CHUNK, CHUNK), BF16)
    return pl.pallas_call(
        functools.partial(_even_kernel, nc=nc, nt=nt),
        grid=(b, nh),
        in_specs=[pl.BlockSpec((1, lc, blk), lambda bi, h: (bi, 0, h)),
                  pl.BlockSpec((1, lt, blk), lambda bi, h: (bi, 0, h)),
                  pl.BlockSpec((1, lc, hd), lambda bi, h: (bi, 0, h)),
                  pl.BlockSpec((1, lt, hd), lambda bi, h: (bi, 0, h)),
                  pl.BlockSpec((1, 3, 3 * hd), lambda bi, h: (h, 0, 0)),
                  pl.BlockSpec((1, 24, hd), lambda bi, h: (h, 0, 0)),
                  pl.BlockSpec((1, hd), lambda bi, h: (0, 0)),
                  pl.BlockSpec((lt, hd), lambda bi, h: (0, 0)),
                  pl.BlockSpec((lt, hd), lambda bi, h: (0, 0))],
        out_specs=[pl.BlockSpec((1, lc, 2 * hd), lambda bi, h: (bi, 0, h)),
                   pl.BlockSpec((1, lt, 2 * hd), lambda bi, h: (bi, 0, h))],
        out_shape=[jax.ShapeDtypeStruct((b, lc, nh * 2 * hd), BF16),
                   jax.ShapeDtypeStruct((b, lt, nh * 2 * hd), BF16)],
        scratch_shapes=[pltpu.VMEM((2 * ntot, CHUNK, CHUNK), F32),
                        pltpu.VMEM((2 * ntot, 2 * CHUNK, CHUNK), BF16),
                        pltpu.VMEM((2 * ntot, 2 * CHUNK, CHUNK), BF16),
                        pltpu.VMEM((2 * ntot, 8, CHUNK), F32),
                        f32_scr, f32_scr, f32_scr,
                        bf16_scr, bf16_scr,
                        f32_scr, f32_scr, f32_scr],
        compiler_params=_cparams(("parallel", "arbitrary")),
        name="deltanet_retention",
    )(pc, pt, gc, gt, conv_h, hp, norm_w.reshape(1, hd), cos_t, sin_t)


def _rot2d(x, c_ref, s1_ref, s2_ref):
    return x * c_ref[...] + pltpu.roll(x, 96, 1) * s1_ref[...] + pltpu.roll(x, 32, 1) * s2_ref[...]


def _attn_kernel(q_ref, kt_ref, vt_ref, kc_ref, vc_ref, qc_ref, qs1_ref, qs2_ref, kc_t, ks1_t, ks2_t,
                 lam_ref, sw_ref, o_ref, kfull, vfull, *, lt, lc, lam_init):
    qi = pl.program_id(2)
    hd = DIFF_HD
    lk = lt + lc

    @pl.when(qi == 0)
    def _():
        for r0 in range(0, lt, 128):
            kr = _rot2d(kt_ref[0, r0:r0 + 128, :].astype(F32), kc_t.at[r0:r0 + 128, :],
                        ks1_t.at[r0:r0 + 128, :], ks2_t.at[r0:r0 + 128, :]).T
            kfull[0, :, r0:r0 + 128] = kr[0:hd, :].astype(BF16)
            kfull[1, :, r0:r0 + 128] = kr[hd:2 * hd, :].astype(BF16)
        for r0 in range(0, lc, 128):
            kr = kc_ref[0, r0:r0 + 128, :].astype(F32).T
            kfull[0, :, lt + r0:lt + r0 + 128] = kr[0:hd, :].astype(BF16)
            kfull[1, :, lt + r0:lt + r0 + 128] = kr[hd:2 * hd, :].astype(BF16)
        lane = lax.broadcasted_iota(jnp.int32, (lk, 128), 1)
        vfull[0:lt, 0:128] = vt_ref[0].astype(vfull.dtype)
        vfull[lt:lk, 0:128] = vc_ref[0].astype(vfull.dtype)
        vfull[:, 128:256] = jnp.where(lane == 0, 1.0, 0.0).astype(BF16)

    lp = lam_ref[...]
    lam = (jnp.exp(jnp.sum(lp[0:1, :] * lp[1:2, :], axis=-1, keepdims=True))
           - jnp.exp(jnp.sum(lp[2:3, :] * lp[3:4, :], axis=-1, keepdims=True)) + lam_init)
    q = (_rot2d(q_ref[0].astype(F32), qc_ref, qs1_ref, qs2_ref) * (hd ** -0.5 * LOG2_E)).astype(BF16)
    tq = q.shape[0]
    units = [(r0, comp) for r0 in range(0, tq, ATTN_SUB) for comp in range(2)]

    def score(r0, comp):
        return _dot(q[r0:r0 + ATTN_SUB, comp * hd:(comp + 1) * hd], kfull[comp])

    def weighted(s):
        p = jnp.exp2(s - jnp.max(s, axis=-1, keepdims=True)).astype(BF16)
        r = _dot(p, vfull[...])
        return r[:, 0:128] / r[:, 128:129]

    scores = [score(*u) for u in units[:SCORE_AHEAD]]
    outs = []
    for i in range(len(units)):
        if i + SCORE_AHEAD < len(units):
            scores.append(score(*units[i + SCORE_AHEAD]))
        outs.append(weighted(scores[i]))
        scores[i] = None
    for j, r0 in enumerate(range(0, tq, ATTN_SUB)):
        o = outs[2 * j] - lam * outs[2 * j + 1]
        o = o * lax.rsqrt(jnp.mean(o * o, axis=-1, keepdims=True) + NORM_EPS) * sw_ref[...] * (1.0 - lam_init)
        o_ref[0, r0:r0 + ATTN_SUB, :] = o.astype(o_ref.dtype)


def _attn_call(pt, pc, tabs, lam_p, subln_w, lam_init):
    b, lt, _ = pt.shape
    lc = pc.shape[1]
    nh = DIFF_HEADS
    tq = min(lt, ATTN_TQ)
    ct, s1t, s2t = tabs
    tab_q = pl.BlockSpec((tq, 128), lambda bi, h, qi: (qi, 0))
    tab_k = pl.BlockSpec((lt, 128), lambda bi, h, qi: (0, 0))
    return pl.pallas_call(
        functools.partial(_attn_kernel, lt=lt, lc=lc, lam_init=lam_init),
        grid=(b, nh, lt // tq),
        in_specs=[pl.BlockSpec((1, tq, 128), lambda bi, h, qi: (bi, qi, h)),
                  pl.BlockSpec((1, lt, 128), lambda bi, h, qi: (bi, 0, nh + h)),
                  pl.BlockSpec((1, lt, 128), lambda bi, h, qi: (bi, 0, 2 * nh + h)),
                  pl.BlockSpec((1, lc, 128), lambda bi, h, qi: (bi, 0, nh + h)),
                  pl.BlockSpec((1, lc, 128), lambda bi, h, qi: (bi, 0, 2 * nh + h)),
                  tab_q, tab_q, tab_q, tab_k, tab_k, tab_k,
                  pl.BlockSpec((4, DIFF_HD), lambda bi, h, qi: (0, 0)),
                  pl.BlockSpec((1, 128), lambda bi, h, qi: (0, 0))],
        out_specs=pl.BlockSpec((1, tq, 128), lambda bi, h, qi: (bi, qi, h)),
        out_shape=jax.ShapeDtypeStruct((b, lt, nh * DIFF_DV), BF16),
        scratch_shapes=[pltpu.VMEM((2, DIFF_HD, lt + lc), BF16), pltpu.VMEM((lt + lc, 256), BF16)],
        compiler_params=_cparams(("parallel", "parallel", "arbitrary")),
        name="diff_attention",
    )(pt, pt, pt, pc, pc, ct, s1t, s2t, ct, s1t, s2t, lam_p, subln_w.reshape(1, 128))


def _even_weights(w_in, conv_w, a_log, dt_bias, ret_decay, w_out):
    nh, hd = DN_HEADS, HEAD_DIM
    o_z, o_a, o_b = 3 * nh * hd, 4 * nh * hd, 4 * nh * hd + 2 * nh
    o_rq = o_b + 2 * nh

    def col(base, h):
        return w_in[:, base + h * hd:base + (h + 1) * hd]

    main, gate, conv_h, hp = [], [], [], []
    d = w_in.shape[0]
    for h in range(nh):
        main += [col(0, h), col(nh * hd, h), col(2 * nh * hd, h), col(o_z, h),
                 col(o_rq, h), col(o_rq + nh * hd, h), col(o_rq + 2 * nh * hd, h), col(o_rq + 3 * nh * hd, h)]
        gate += [w_in[:, o_a + h:o_a + h + 1], w_in[:, o_a + nh + h:o_a + nh + h + 1],
                 w_in[:, o_b + h:o_b + h + 1], w_in[:, o_b + nh + h:o_b + nh + h + 1],
                 jnp.zeros((d, hd - 4), w_in.dtype)]
        conv_h.append(jnp.concatenate([conv_w[:, h * hd:(h + 1) * hd], conv_w[:, (nh + h) * hd:(nh + h + 1) * hd],
                                       conv_w[:, (2 * nh + h) * hd:(2 * nh + h + 1) * hd]], axis=1))
        ones = jnp.ones((hd,), F32)
        zero = jnp.zeros((hd,), F32)
        hp.append(jnp.stack([ones * a_log[0, h], ones * a_log[1, h]] + [zero] * 6
                            + [ones * dt_bias[0, h], ones * dt_bias[1, h]] + [zero] * 6
                            + [ones * ret_decay[0, h], ones * ret_decay[1, h]] + [zero] * 6))
    w_main = jnp.concatenate(main, axis=1).astype(BF16)
    w_gate = jnp.concatenate(gate, axis=1).astype(BF16)
    w_o = jnp.concatenate([w_out[(j * nh + h) * hd:(j * nh + h + 1) * hd] for h in range(nh) for j in range(2)],
                          axis=0).astype(BF16)
    return w_main, w_gate, jnp.stack(conv_h).astype(F32), jnp.stack(hp).astype(F32), w_o


def _gate_up_weights(w_gate, w_up):
    d, dff = w_gate.shape
    nf = dff // FFN_TF
    both = jnp.concatenate([w_gate.reshape(d, nf, FFN_TF), w_up.reshape(d, nf, FFN_TF)], axis=2)
    return both.reshape(d, nf * 2 * FFN_TF).astype(BF16)


def _ret_tables(l):
    half = HEAD_DIM // 2
    inv = ROPE_BASE ** (-jnp.arange(half, dtype=F32) / half)
    ang = jnp.arange(l).astype(F32)[:, None] * inv[None]
    cs, sn = jnp.cos(ang), jnp.sin(ang)
    return jnp.concatenate([cs, cs], -1), jnp.concatenate([-sn, sn], -1)


def _diff_tables(l):
    n = DIFF_HD // 4
    pos = jnp.arange(l)
    inv = ROPE_BASE ** (-jnp.arange(n, dtype=F32) / n)
    ang = jnp.concatenate([(pos // GRID_W).astype(F32)[:, None] * inv[None],
                           (pos % GRID_W).astype(F32)[:, None] * inv[None]], -1)
    cs, sn, zero = jnp.cos(ang), jnp.sin(ang), jnp.zeros_like(ang)
    c64 = jnp.concatenate([cs, cs], -1)
    s1 = jnp.concatenate([-sn, zero], -1)
    s2 = jnp.concatenate([zero, sn], -1)
    return tuple(jnp.concatenate([t, t], -1) for t in (c64, s1, s2))


def kernel(x, c, ctx, c_ctx, mod_w, mod_b, ln_g, ln_b, e_w_in, e_conv, e_a_log, e_dt_bias, e_norm_w, e_ret_decay,
           e_w_out, o_w_qkv, o_lambda, o_subln_w, o_w_out, f_w_gate, f_w_up, f_conv, f_w_down):
    depth = mod_w.shape[0]
    b, l, d = x.shape
    lc = ctx.shape[1]
    alpha = (2 * depth) ** 0.25
    n_rows = -(-(b + 1) // 8) * 8
    cc = jnp.concatenate([c, c_ctx[None, :], jnp.zeros((n_rows - b - 1, d), F32)], axis=0)
    mod = _mod_call(cc, mod_w, mod_b).reshape(depth, n_rows, 1, 6 * d)[:, :b + 1]
    ret_cos, ret_sin = _ret_tables(l)
    diff_tabs = _diff_tables(l)
    for li in range(depth):
        last = li == depth - 1
        i = li // 2
        mod_l = mod[li]
        if li % 2 == 0:
            w_main, w_gate, conv_h, hp, w_o = _even_weights(e_w_in[i], e_conv[i], e_a_log[i], e_dt_bias[i],
                                                            e_ret_decay[i], e_w_out[i])
            pc, gc = _proj_call(ctx, mod_l, 0, True, [w_main, w_gate], [BF16, F32], "even_proj_ctx")
            pt, gt = _proj_call(x, mod_l, 0, False, [w_main, w_gate], [BF16, F32], "even_proj_lat")
            yc, yt = _even_call(pc, pt, gc, gt, conv_h, hp, e_norm_w[i], ret_cos, ret_sin)
        else:
            lam_init = 0.8 - 0.6 * math.exp(-0.3 * li)
            w_qkv = o_w_qkv[i].astype(BF16)
            (pc,) = _proj_call(ctx, mod_l, 0, True, [w_qkv], [BF16], "odd_proj_ctx")
            (pt,) = _proj_call(x, mod_l, 0, False, [w_qkv], [BF16], "odd_proj_lat")
            yt = _attn_call(pt, pc, diff_tabs, o_lambda[i], o_subln_w[i], lam_init)
            w_o = o_w_out[i].astype(BF16)
            yc = None
            if not last:
                raise NotImplementedError("context update after a differential-attention layer")
        wgu, wd = _gate_up_weights(f_w_gate[li], f_w_up[li]), f_w_down[li].astype(BF16)
        wcv = f_conv[li].reshape(9, -1)
        x1, hf = _outln_call(yt, x, mod_l, 2, False, w_o, ln_g[li, 0], ln_b[li, 0], alpha, 3, "mix_out_lat")
        yf = _ffn_call(hf, wgu, wcv, wd, GRID_W, "ffn_lat")
        x = _outln_call(yf, x1, mod_l, 5, False, None, ln_g[li, 1], ln_b[li, 1], alpha, None, "ffn_ln_lat")
        if not last:
            c1, hcf = _outln_call(yc, ctx, mod_l, 2, True, w_o, ln_g[li, 0], ln_b[li, 0], alpha, 3, "mix_out_ctx")
            ycf = _ffn_call(hcf, wgu, wcv, wd, lc, "ffn_ctx")
            ctx = _outln_call(ycf, c1, mod_l, 5, True, None, ln_g[li, 1], ln_b[li, 1], alpha, None, "ffn_ln_ctx")
    return x
```

```python
import functools
import math

import jax
import jax.numpy as jnp
from jax import lax
from jax.experimental import pallas as pl
from jax.experimental.pallas import tpu as pltpu

F32 = jnp.float32
BF16 = jnp.bfloat16

GRID_W = 64
DN_HEADS = 4
RET_HEADS = 4
HEAD_DIM = 128
DIFF_HEADS = 8
DIFF_HD = 64
DIFF_DV = 2 * DIFF_HD
ROPE_BASE = 10000.0
LN_EPS = 1e-5
NORM_EPS = 1e-6

CHUNK = 128
INV_BASE = 16
PAR_CHUNKS = 4
PROJ_ROWS = 1024
ATTN_TQ = 1024
ATTN_SUB = 128
SCORE_AHEAD = 3
LOG2_E = 1.4426950408889634
GATE_AHEAD = 2
FFN_ROWS = 512
FFN_LEAD_EDGES = (128, 256)
FFN_TF = 256
FFN_SUB = 64
BF16_ROWS = 16
VMEM_LIMIT = 56 * 1024 * 1024


def _dot(a, b):
    return jnp.dot(a, b, preferred_element_type=F32)


def _silu(x):
    return x * jax.nn.sigmoid(x)


def _softplus(x):
    return jnp.maximum(x, 0.0) + jnp.log1p(jnp.exp(-jnp.abs(x)))


def _cparams(sem):
    return pltpu.CompilerParams(dimension_semantics=sem, vmem_limit_bytes=VMEM_LIMIT)


def _mod_kernel(cc_ref, w_ref, b_ref, o_ref):
    a = _silu(cc_ref[...])
    w = w_ref[0]
    a_hi = a.astype(BF16)
    a_lo = (a - a_hi.astype(F32)).astype(BF16)
    w_hi = w.astype(BF16)
    w_lo = (w - w_hi.astype(F32)).astype(BF16)
    o_ref[0] = _dot(a_hi, w_hi) + _dot(a_hi, w_lo) + _dot(a_lo, w_hi) + b_ref[0]


def _mod_call(cc, mod_w, mod_b):
    depth, d, n = mod_w.shape
    r = cc.shape[0]
    tn = min(n, 1536)
    return pl.pallas_call(
        _mod_kernel,
        grid=(depth, n // tn),
        in_specs=[pl.BlockSpec((r, d), lambda l, j: (0, 0)),
                  pl.BlockSpec((1, d, tn), lambda l, j: (l, 0, j)),
                  pl.BlockSpec((1, 1, tn), lambda l, j: (l, 0, j))],
        out_specs=pl.BlockSpec((1, r, tn), lambda l, j: (l, 0, j)),
        out_shape=jax.ShapeDtypeStruct((depth, r, n), F32),
        compiler_params=_cparams(("parallel", "parallel")),
        name="adaln_maps",
    )(cc, mod_w, mod_b.reshape(depth, 1, n))


def _mod_spec(d, k, is_ctx, nb, ngrid):
    if ngrid == 2:
        imap = (lambda b, i: (nb, 0, k)) if is_ctx else (lambda b, i: (b, 0, k))
    else:
        imap = (lambda b, i, j: (nb, 0, k)) if is_ctx else (lambda b, i, j: (b, 0, k))
    return pl.BlockSpec((1, 1, d), imap)


def _proj_kernel(x_ref, sh_ref, sc_ref, *refs, n_w, tn):
    w_refs, o_refs = refs[:n_w], refs[n_w:]
    h = (x_ref[0] * (1.0 + sc_ref[0]) + sh_ref[0]).astype(BF16)
    for w_ref, o_ref in zip(w_refs, o_refs):
        n = w_ref.shape[1]
        for n0 in range(0, n, tn):
            n1 = min(n0 + tn, n)
            o_ref[0, :, n0:n1] = _dot(h, w_ref[:, n0:n1]).astype(o_ref.dtype)


def _proj_call(x, mod_l, k_shift, is_ctx, weights, out_dtypes, name):
    b, l, d = x.shape
    nb = mod_l.shape[0] - 1
    tm = min(l, PROJ_ROWS)
    in_specs = [pl.BlockSpec((1, tm, d), lambda bi, i: (bi, i, 0)),
                _mod_spec(d, k_shift, is_ctx, nb, 2),
                _mod_spec(d, k_shift + 1, is_ctx, nb, 2)]
    out_specs, out_shape = [], []
    for w, dt in zip(weights, out_dtypes):
        n = w.shape[1]
        in_specs.append(pl.BlockSpec((d, n), lambda bi, i: (0, 0)))
        out_specs.append(pl.BlockSpec((1, tm, n), lambda bi, i: (bi, i, 0)))
        out_shape.append(jax.ShapeDtypeStruct((b, l, n), dt))
    return pl.pallas_call(
        functools.partial(_proj_kernel, n_w=len(weights), tn=512),
        grid=(b, l // tm),
        in_specs=in_specs, out_specs=out_specs, out_shape=out_shape,
        compiler_params=_cparams(("parallel", "parallel")),
        name=name,
    )(x, mod_l, mod_l, *weights)


def _outln_kernel(*refs, alpha, has_w, has_h):
    refs = list(refs)
    y_ref, x_ref, g_ref = refs[:3]
    refs = refs[3:]
    w_ref = refs.pop(0) if has_w else None
    lng_ref, lnb_ref = refs[:2]
    refs = refs[2:]
    if has_h:
        sh_ref, sc_ref, o_ref, h_ref = refs
    else:
        (o_ref,) = refs
    t = _dot(y_ref[0], w_ref[...]) if has_w else y_ref[0].astype(F32)
    z = alpha * x_ref[0] + g_ref[0] * t
    mu = jnp.mean(z, axis=-1, keepdims=True)
    zc = z - mu
    var = jnp.mean(zc * zc, axis=-1, keepdims=True)
    o = zc * lax.rsqrt(var + LN_EPS) * lng_ref[...] + lnb_ref[...]
    o_ref[0] = o
    if has_h:
        h_ref[0] = (o * (1.0 + sc_ref[0]) + sh_ref[0]).astype(BF16)


def _outln_call(y, x, mod_l, k_gate, is_ctx, w, ln_g, ln_b, alpha, k_shift_next, name):
    b, l, d = x.shape
    kdim = y.shape[-1]
    nb = mod_l.shape[0] - 1
    tm = min(l, 512)
    has_w, has_h = w is not None, k_shift_next is not None
    args = [y, x, mod_l]
    in_specs = [pl.BlockSpec((1, tm, kdim), lambda bi, i: (bi, i, 0)),
                pl.BlockSpec((1, tm, d), lambda bi, i: (bi, i, 0)),
                _mod_spec(d, k_gate, is_ctx, nb, 2)]
    if has_w:
        args.append(w)
        in_specs.append(pl.BlockSpec((kdim, d), lambda bi, i: (0, 0)))
    args += [ln_g.reshape(1, d), ln_b.reshape(1, d)]
    in_specs += [pl.BlockSpec((1, d), lambda bi, i: (0, 0))] * 2
    out_specs = [pl.BlockSpec((1, tm, d), lambda bi, i: (bi, i, 0))]
    out_shape = [jax.ShapeDtypeStruct((b, l, d), F32)]
    if has_h:
        args += [mod_l, mod_l]
        in_specs += [_mod_spec(d, k_shift_next, is_ctx, nb, 2), _mod_spec(d, k_shift_next + 1, is_ctx, nb, 2)]
        out_specs.append(pl.BlockSpec((1, tm, d), lambda bi, i: (bi, i, 0)))
        out_shape.append(jax.ShapeDtypeStruct((b, l, d), BF16))
    res = pl.pallas_call(
        functools.partial(_outln_kernel, alpha=alpha, has_w=has_w, has_h=has_h),
        grid=(b, l // tm),
        in_specs=in_specs, out_specs=out_specs, out_shape=out_shape,
        compiler_params=_cparams(("parallel", "parallel")),
        name=name,
    )(*args)
    return res if has_h else res[0]


def _ffn_kernel(h_ref, wgu_ref, wc_ref, wd_ref, x_ref, g_ref, lng_ref, lnb_ref, o_ref, apad, ubuf,
                *, l, w, vertical, pad, rb, alpha):
    f = pl.program_id(1)
    tf = wd_ref.shape[0]

    @pl.when(f == 0)
    def _():
        o_ref[0] = jnp.zeros(o_ref.shape[1:], F32)
        apad[0:pad, :] = jnp.zeros((pad, tf), F32)
        apad[pad + l:pad + l + pad, :] = jnp.zeros((pad, tf), F32)

    edges = [e for e in FFN_LEAD_EDGES if e < rb and e < l] + list(range(rb, l, rb)) + [l] if l > rb else [l]
    blocks = list(zip([0] + edges[:-1], edges))
    nblk = len(blocks)
    dis = (-1, 0, 1) if vertical else (0,)
    wrow = [wc_ref[k:k + 1, :] for k in range(9)]
    row8 = lax.broadcasted_iota(jnp.int32, (8, tf), 0)

    def gate(i):
        lo, hi = blocks[i]
        gu = _dot(h_ref[0, lo:hi, :], wgu_ref[...])
        apad[pad + lo:pad + hi, :] = gu[:, 0:tf]
        ubuf[lo:hi, :] = gu[:, tf:2 * tf]

    def colsum(dj, lo, n):
        tot = None
        for di in dis:
            term = apad[lo + di * w:lo + di * w + n, :] * wrow[(di + 1) * 3 + dj + 1]
            tot = term if tot is None else tot + term
        return tot

    def down(i, t):
        lo, hi = blocks[i]
        o_ref[0, lo:hi, :] += _dot(t, wd_ref[...])

    for i in range(min(GATE_AHEAD, nblk)):
        gate(i)
    t_prev = None
    for i in range(nblk):
        if i + GATE_AHEAD < nblk:
            gate(i + GATE_AHEAD)
        r0, r1 = blocks[i]
        base = pad + r0
        if t_prev is not None:
            down(i - 1, t_prev)
        ts = []
        for s0 in range(0, r1 - r0, FFN_SUB):
            lo = base + s0
            n_ext = FFN_SUB + 16
            left = pltpu.roll(colsum(-1, lo - 8, n_ext), 1, 0)[8:8 + FFN_SUB, :]
            right = pltpu.roll(colsum(1, lo - 8, n_ext), n_ext - 1, 0)[8:8 + FFN_SUB, :]
            if (r0 + s0) % w == 0:
                left = jnp.concatenate([jnp.where(row8 == 0, 0.0, left[0:8, :]), left[8:, :]], axis=0)
            if (r0 + s0 + FFN_SUB) % w == 0:
                right = jnp.concatenate([right[:FFN_SUB - 8, :], jnp.where(row8 == 7, 0.0, right[FFN_SUB - 8:, :])],
                                        axis=0)
            conv = colsum(0, lo, FFN_SUB) + left + right
            ts.append((_silu(conv) * ubuf[r0 + s0:r0 + s0 + FFN_SUB, :]).astype(BF16))
        t_prev = jnp.concatenate(ts, axis=0)
    down(nblk - 1, t_prev)

    @pl.when(f == pl.num_programs(1) - 1)
    def _():
        for lo in range(0, l, FFN_SUB):
            z = alpha * x_ref[0, lo:lo + FFN_SUB, :] + g_ref[0] * o_ref[0, lo:lo + FFN_SUB, :]
            mu = jnp.mean(z, axis=-1, keepdims=True)
            zc = z - mu
            var = jnp.mean(zc * zc, axis=-1, keepdims=True)
            o_ref[0, lo:lo + FFN_SUB, :] = zc * lax.rsqrt(var + LN_EPS) * lng_ref[...] + lnb_ref[...]


def _ffn_call(hf, w_gate_up, w_conv, w_down, grid_w, vertical, x, mod_l, k_gate, is_ctx, ln_g, ln_b, alpha, name):
    b, l, d = hf.shape
    dff = w_down.shape[0]
    tf = FFN_TF
    rb = min(l, FFN_ROWS)
    pad = grid_w + 8 if vertical else 8
    assert grid_w % FFN_SUB == 0 and l % rb == 0 and rb % FFN_SUB == 0 and all(e % FFN_SUB == 0 for e in FFN_LEAD_EDGES)
    return pl.pallas_call(
        functools.partial(_ffn_kernel, l=l, w=grid_w, vertical=vertical, pad=pad, rb=rb, alpha=alpha),
        grid=(b, dff // tf),
        in_specs=[pl.BlockSpec((1, l, d), lambda bi, f: (bi, 0, 0)),
                  pl.BlockSpec((d, 2 * tf), lambda bi, f: (0, f)),
                  pl.BlockSpec((9, tf), lambda bi, f: (0, f)),
                  pl.BlockSpec((tf, d), lambda bi, f: (f, 0)),
                  pl.BlockSpec((1, l, d), lambda bi, f: (bi, 0, 0)),
                  _mod_spec(d, k_gate, is_ctx, mod_l.shape[0] - 1, 2),
                  pl.BlockSpec((1, d), lambda bi, f: (0, 0)),
                  pl.BlockSpec((1, d), lambda bi, f: (0, 0))],
        out_specs=pl.BlockSpec((1, l, d), lambda bi, f: (bi, 0, 0)),
        out_shape=jax.ShapeDtypeStruct((b, l, d), F32),
        scratch_shapes=[pltpu.VMEM((l + 2 * pad, tf), F32), pltpu.VMEM((l, tf), F32)],
        compiler_params=_cparams(("parallel", "arbitrary")),
        name=name,
    )(hf, w_gate_up, w_conv, w_down, x, mod_l, ln_g.reshape(1, d), ln_b.reshape(1, d))


def _even_kernel(pc_ref, pt_ref, gc_ref, gt_ref, cw_ref, hp_ref, nw_ref, cos_ref, sin_ref,
                 yc_ref, yt_ref,
                 u_s, wq_s, qkk_s, cd_s, ri_s, kvf_s, kvb_s, rqf_s, rqb_s, of_s, ob_s, rb_s, *, nc, nt):
    c = CHUNK
    hd = HEAD_DIM
    ntot = nc + nt
    ri = lax.broadcasted_iota(jnp.int32, (c, c), 0)
    ci = lax.broadcasted_iota(jnp.int32, (c, c), 1)
    rif = ri.astype(F32)
    cif = ci.astype(F32)
    eye = (ri == ci).astype(F32)

    def same_block(size):
        sh = size.bit_length() - 1
        return jnp.right_shift(ri, sh) == jnp.right_shift(ci, sh)

    diag_mask = same_block(INV_BASE).astype(F32)
    off_masks = []
    size = INV_BASE
    while size < c:
        off_masks.append(jnp.where(same_block(2 * size), 1.0, 0.0) - jnp.where(same_block(size), 1.0, 0.0))
        size *= 2
    hp = hp_ref[0]
    a_log8, dt8 = hp[0:8, :], hp[8:16, :]
    lg_f = -jnp.exp(hp[16:17, :])
    lg_b = -jnp.exp(hp[17:18, :])
    dec_bi = (jnp.where(ri >= ci, jnp.exp(lg_f * (rif - cif)), 0.0)
              + jnp.where(ci >= ri, jnp.exp(lg_b * (cif - rif)), 0.0))
    gq_f = jnp.exp(lg_f * (rif + 1.0))
    gq_b = jnp.exp(lg_b * (c - rif))
    gk_f = jnp.exp(lg_f * (c - 1.0 - cif))
    gk_b = jnp.exp(lg_b * cif)
    cd_f = jnp.exp(lg_f * c)
    cd_b = jnp.exp(lg_b * c)
    row8 = lax.broadcasted_iota(jnp.int32, (8, c), 0)
    lane8 = lax.broadcasted_iota(jnp.int32, (8, c), 1)

    def gate_rows(g):
        gt8 = g.T[0:8, :]
        la = -jnp.exp(a_log8) * _softplus(gt8 + dt8)
        pre, suf = la, la
        s = 1
        while s < c:
            pre = pre + jnp.where(lane8 >= s, pltpu.roll(pre, s, 1), 0.0)
            suf = suf + jnp.where(lane8 < c - s, pltpu.roll(suf, c - s, 1), 0.0)
            s *= 2
        cum = jnp.where(row8 == 0, pre, suf)
        ecum = pltpu.roll(jnp.exp(cum), 4, 0)
        return jnp.where(row8 < 2, cum, jnp.where(row8 < 4, jax.nn.sigmoid(gt8), jnp.where(row8 < 6, ecum, 0.0)))

    def chunk_prep(src_ref, gsrc_ref, m, ns, off, rot):
        ln = ns * c
        r0 = pl.multiple_of(m * c, c)
        rows = pl.ds(r0, c)
        x = src_ref[0, rows, 0:3 * hd].astype(F32)
        pstart = pl.multiple_of(jnp.maximum(r0 - BF16_ROWS, 0), BF16_ROWS)
        nstart = pl.multiple_of(jnp.minimum(r0 + c, ln - BF16_ROWS), BF16_ROWS)
        prev = src_ref[0, pl.ds(pstart, BF16_ROWS), 0:3 * hd].astype(F32)[BF16_ROWS - 1:BF16_ROWS, :]
        nxt = src_ref[0, pl.ds(nstart, BF16_ROWS), 0:3 * hd].astype(F32)[0:1, :]
        prev = prev * jnp.where(m > 0, 1.0, 0.0)
        nxt = nxt * jnp.where(m < ns - 1, 1.0, 0.0)
        rr = lax.broadcasted_iota(jnp.int32, (c, 3 * hd), 0)
        xm1 = jnp.where(rr == 0, prev, pltpu.roll(x, 1, 0))
        xp1 = jnp.where(rr == c - 1, nxt, pltpu.roll(x, c - 1, 0))
        cw = cw_ref[0]
        y = _silu(xm1 * cw[0:1, :] + x * cw[1:2, :] + xp1 * cw[2:3, :])
        q = y[:, 0:hd]
        k = y[:, hd:2 * hd]
        q = q * (lax.rsqrt(jnp.sum(q * q, axis=-1, keepdims=True) + NORM_EPS) * (hd ** -0.5))
        k = k * lax.rsqrt(jnp.sum(k * k, axis=-1, keepdims=True) + NORM_EPS)
        xt8 = gate_rows(gsrc_ref[0, rows, :])
        xg = jnp.concatenate([xt8, jnp.zeros((c - 8, c), F32)], axis=0).T
        rq = src_ref[0, rows, 4 * hd:5 * hd].astype(F32)
        rk = src_ref[0, rows, 5 * hd:6 * hd].astype(F32)
        if rot:
            cs, sn = cos_ref[rows, :], sin_ref[rows, :]
            rq = rq * cs + pltpu.roll(rq, hd // 2, 1) * sn
            rk = rk * cs + pltpu.roll(rk, hd // 2, 1) * sn
        return dict(n=m + off, q=q, k=k, kt=k.T, v=y[:, 2 * hd:3 * hd], xt8=xt8, xg=xg,
                    rq=rq, rkt=(rk * (hd ** -0.5)).T, rv=src_ref[0, rows, 6 * hd:7 * hd])

    def chunks_work(preps):
        for pr in preps:
            pr["kt16"] = pr["kt"].astype(BF16)
        kks = [_dot(pr["k"].astype(BF16), pr["kt16"]) for pr in preps]
        qks = [_dot(pr["q"].astype(BF16), pr["kt16"]) for pr in preps]
        scs = [_dot(pr["rq"].astype(BF16), pr["rkt"].astype(BF16)) for pr in preps]
        chains = []
        for pr, kk, qk in zip(preps, kks, qks):
            for d in range(2):
                xg, xt8 = pr["xg"], pr["xt8"]
                g, beta, eg = xg[:, d:d + 1], xg[:, 2 + d:3 + d], xg[:, 4 + d:5 + d]
                g_row = xt8[d:d + 1, :]
                last = 0 if d else c - 1
                g_last = xt8[d:d + 1, last:last + 1]
                e = jnp.exp(jnp.minimum(g - g_row, 0.0))
                incl = (ri <= ci) if d else (ri >= ci)
                strict = (ri < ci) if d else (ri > ci)
                nm = -(beta * kk) * jnp.where(strict, e, 0.0)
                nd = nm * diag_mask
                chains.append(dict(pr=pr, d=d, nm=nm, nd16=nd.astype(BF16), p=eye + nd, beta=beta, eg=eg,
                                   qkm=qk * jnp.where(incl, e, 0.0), kgt=pr["kt"] * jnp.exp(g_last - g_row),
                                   cd=jnp.broadcast_to(jnp.exp(g_last), (8, c))))
        for ch in chains:
            ch["npow"] = _dot(ch["nd16"], ch["nd16"])
        lvl = 2
        while lvl * 2 < INV_BASE:
            rs = [_dot(jnp.concatenate([ch["p"], ch["npow"]], axis=0).astype(BF16), ch["npow"].astype(BF16))
                  for ch in chains]
            for ch, r in zip(chains, rs):
                ch["p"] = ch["p"] + r[0:c, :]
                ch["npow"] = r[c:, :]
            lvl *= 2
        rs = [_dot(ch["p"].astype(BF16), ch["npow"].astype(BF16)) for ch in chains]
        for ch, r in zip(chains, rs):
            ch["p"] = ch["p"] + r
        for om in off_masks:
            xs = [_dot(ch["p"].astype(BF16), (ch["nm"] * om).astype(BF16)) for ch in chains]
            rs = [_dot(x.astype(BF16), ch["p"].astype(BF16)) for ch, x in zip(chains, xs)]
            for ch, r in zip(chains, rs):
                ch["p"] = ch["p"] + r
        sols = [_dot(ch["p"].astype(BF16),
                     jnp.concatenate([ch["pr"]["v"] * ch["beta"], ch["pr"]["k"] * (ch["beta"] * ch["eg"])],
                                     axis=1).astype(BF16)) for ch in chains]
        rets = [_dot(jnp.concatenate([sc * dec_bi, pr["rkt"] * gk_f, pr["rkt"] * gk_b], axis=0).astype(BF16), pr["rv"])
                for pr, sc in zip(preps, scs)]
        for ch, sol in zip(chains, sols):
            idx = ch["d"] * ntot + ch["pr"]["n"]
            u_s[idx] = sol[:, 0:hd]
            wq_s[idx] = jnp.concatenate([sol[:, hd:2 * hd], ch["pr"]["q"] * ch["eg"]], axis=0).astype(BF16)
            qkk_s[idx] = jnp.concatenate([ch["qkm"], ch["kgt"]], axis=0).astype(BF16)
            cd_s[idx] = ch["cd"]
        for pr, r in zip(preps, rets):
            n = pr["n"]
            ri_s[n] = r[0:c, :]
            kvf_s[n] = r[c:2 * c, :]
            kvb_s[n] = r[2 * c:3 * c, :]
            rqf_s[n] = (pr["rq"] * gq_f).astype(BF16)
            rqb_s[n] = (pr["rq"] * gq_b).astype(BF16)

    def segment(src_ref, gsrc_ref, ns, off, rot):
        par = math.gcd(ns, PAR_CHUNKS)

        def body(j, carry):
            chunks_work([chunk_prep(src_ref, gsrc_ref, j * par + t, ns, off, rot) for t in range(par)])
            return carry

        lax.fori_loop(0, ns // par, body, 0)

    segment(pc_ref, gc_ref, nc, 0, False)
    segment(pt_ref, gt_ref, nt, nc, True)

    def emit(src_ref, dst_ref, m, o, r):
        rows = pl.ds(pl.multiple_of(m * c, c), c)
        zg = src_ref[0, rows, 3 * hd:4 * hd].astype(F32)
        dn = o * lax.rsqrt(jnp.mean(o * o, axis=-1, keepdims=True) + NORM_EPS) * nw_ref[...] * _silu(zg)
        mu = jnp.mean(r, axis=-1, keepdims=True)
        rc = r - mu
        rg = src_ref[0, rows, 7 * hd:8 * hd].astype(F32)
        rt = rc * lax.rsqrt(jnp.mean(rc * rc, axis=-1, keepdims=True) + NORM_EPS) * _silu(rg)
        dst_ref[0, rows, 0:hd] = dn.astype(dst_ref.dtype)
        dst_ref[0, rows, hd:2 * hd] = rt.astype(dst_ref.dtype)

    def step(i, carry, tail):
        s_f, s_b, t_f, t_b = carry
        nf = i
        nbk = jnp.where(i < nc, nc - 1 - i, ntot - 1 - (i - nc))
        idxs = (nf, ntot + nbk)
        r1s = [_dot(wq_s[idx], st.astype(BF16)) for idx, st in zip(idxs, (s_f, s_b))]
        r_f = ri_s[nf] + _dot(rqf_s[nf], t_f.astype(BF16))
        r_b = _dot(rqb_s[nbk], t_b.astype(BF16))
        r2s = [_dot(qkk_s[idx], (u_s[idx] - r1[0:c, :]).astype(BF16)) for idx, r1 in zip(idxs, r1s)]
        o_f, o_b = [r1[c:, :] + r2[0:c, :] for r1, r2 in zip(r1s, r2s)]
        s_f, s_b = [cd_s[idx][0:1, :] * st + r2[c:, :] for idx, st, r2 in zip(idxs, (s_f, s_b), r2s)]
        t_f = cd_f * t_f + kvf_s[nf]
        t_b = cd_b * t_b + kvb_s[nbk]
        if tail:
            other_b, other_rb, other_f, other_rf = ob_s[nf], rb_s[nf], of_s[nbk], ri_s[nbk]
            emit(pt_ref, yt_ref, nf - nc, o_f + other_b, r_f + other_rb)
            emit(pt_ref, yt_ref, nbk - nc, other_f + o_b, other_rf + r_b)
        else:
            of_s[nf] = o_f
            ob_s[nbk] = o_b
            ri_s[nf] = r_f
            rb_s[nbk] = r_b
        return s_f, s_b, t_f, t_b

    n_head = nc + nt // 2
    z = jnp.zeros((hd, hd), F32)
    carry = lax.fori_loop(0, n_head, functools.partial(step, tail=False), (z, z, z, z))
    lax.fori_loop(n_head, ntot, functools.partial(step, tail=True), carry)

    def ctx_finish(m, carry):
        emit(pc_ref, yc_ref, m, of_s[m] + ob_s[m], ri_s[m] + rb_s[m])
        return carry

    lax.fori_loop(0, nc, ctx_finish, 0)


def _even_call(pc, pt, gc, gt, conv_h, hp, norm_w, cos_t, sin_t):
    b, lc, _ = pc.shape
    lt = pt.shape[1]
    nc, nt = lc // CHUNK, lt // CHUNK
    assert lc % CHUNK == 0 and lt % (2 * CHUNK) == 0
    ntot = nc + nt
    nh = DN_HEADS
    hd = HEAD_DIM
    blk = 8 * hd
    f32_scr = pltpu.VMEM((ntot, CHUNK, CHUNK), F32)
    bf16_scr = pltpu.VMEM((ntot, CHUNK, CHUNK), BF16)
    return pl.pallas_call(
        functools.partial(_even_kernel, nc=nc, nt=nt),
        grid=(b, nh),
        in_specs=[pl.BlockSpec((1, lc, blk), lambda bi, h: (bi, 0, h)),
                  pl.BlockSpec((1, lt, blk), lambda bi, h: (bi, 0, h)),
                  pl.BlockSpec((1, lc, hd), lambda bi, h: (bi, 0, h)),
                  pl.BlockSpec((1, lt, hd), lambda bi, h: (bi, 0, h)),
                  pl.BlockSpec((1, 3, 3 * hd), lambda bi, h: (h, 0, 0)),
                  pl.BlockSpec((1, 24, hd), lambda bi, h: (h, 0, 0)),
                  pl.BlockSpec((1, hd), lambda bi, h: (0, 0)),
                  pl.BlockSpec((lt, hd), lambda bi, h: (0, 0)),
                  pl.BlockSpec((lt, hd), lambda bi, h: (0, 0))],
        out_specs=[pl.BlockSpec((1, lc, 2 * hd), lambda bi, h: (bi, 0, h)),
                   pl.BlockSpec((1, lt, 2 * hd), lambda bi, h: (bi, 0, h))],
        out_shape=[jax.ShapeDtypeStruct((b, lc, nh * 2 * hd), BF16),
                   jax.ShapeDtypeStruct((b, lt, nh * 2 * hd), BF16)],
        scratch_shapes=[pltpu.VMEM((2 * ntot, CHUNK, CHUNK), F32),
                        pltpu.VMEM((2 * ntot, 2 * CHUNK, CHUNK), BF16),
                        pltpu.VMEM((2 * ntot, 2 * CHUNK, CHUNK), BF16),
                        pltpu.VMEM((2 * ntot, 8, CHUNK), F32),
                        f32_scr, f32_scr, f32_scr,
                        bf16_scr, bf16_scr,
                        f32_scr, f32_scr, f32_scr],
        compiler_params=_cparams(("parallel", "arbitrary")),
        name="deltanet_retention",
    )(pc, pt, gc, gt, conv_h, hp, norm_w.reshape(1, hd), cos_t, sin_t)


def _rot2d(x, c_ref, s1_ref, s2_ref):
    return x * c_ref[...] + pltpu.roll(x, 96, 1) * s1_ref[...] + pltpu.roll(x, 32, 1) * s2_ref[...]


def _attn_kernel(q_ref, kt_ref, vt_ref, kc_ref, vc_ref, qc_ref, qs1_ref, qs2_ref, kc_t, ks1_t, ks2_t,
                 lam_ref, sw_ref, o_ref, kfull, vfull, *, lt, lc, lam_init):
    qi = pl.program_id(2)
    hd = DIFF_HD
    lk = lt + lc

    @pl.when(qi == 0)
    def _():
        for r0 in range(0, lt, 128):
            kr = _rot2d(kt_ref[0, r0:r0 + 128, :].astype(F32), kc_t.at[r0:r0 + 128, :],
                        ks1_t.at[r0:r0 + 128, :], ks2_t.at[r0:r0 + 128, :]).T
            kfull[0, :, r0:r0 + 128] = kr[0:hd, :].astype(BF16)
            kfull[1, :, r0:r0 + 128] = kr[hd:2 * hd, :].astype(BF16)
        for r0 in range(0, lc, 128):
            kr = kc_ref[0, r0:r0 + 128, :].astype(F32).T
            kfull[0, :, lt + r0:lt + r0 + 128] = kr[0:hd, :].astype(BF16)
            kfull[1, :, lt + r0:lt + r0 + 128] = kr[hd:2 * hd, :].astype(BF16)
        lane = lax.broadcasted_iota(jnp.int32, (lk, 128), 1)
        vfull[0:lt, 0:128] = vt_ref[0].astype(vfull.dtype)
        vfull[lt:lk, 0:128] = vc_ref[0].astype(vfull.dtype)
        vfull[:, 128:256] = jnp.where(lane == 0, 1.0, 0.0).astype(BF16)

    lp = lam_ref[...]
    lam = (jnp.exp(jnp.sum(lp[0:1, :] * lp[1:2, :], axis=-1, keepdims=True))
           - jnp.exp(jnp.sum(lp[2:3, :] * lp[3:4, :], axis=-1, keepdims=True)) + lam_init)
    q = (_rot2d(q_ref[0].astype(F32), qc_ref, qs1_ref, qs2_ref) * (hd ** -0.5 * LOG2_E)).astype(BF16)
    tq = q.shape[0]
    units = [(r0, comp) for r0 in range(0, tq, ATTN_SUB) for comp in range(2)]

    def score(r0, comp):
        return _dot(q[r0:r0 + ATTN_SUB, comp * hd:(comp + 1) * hd], kfull[comp])

    def weighted(s):
        p = jnp.exp2(s - jnp.max(s, axis=-1, keepdims=True)).astype(BF16)
        r = _dot(p, vfull[...])
        return r[:, 0:128] / r[:, 128:129]

    scores = [score(*u) for u in units[:SCORE_AHEAD]]
    outs = []
    for i in range(len(units)):
        if i + SCORE_AHEAD < len(units):
            scores.append(score(*units[i + SCORE_AHEAD]))
        outs.append(weighted(scores[i]))
        scores[i] = None
    for j, r0 in enumerate(range(0, tq, ATTN_SUB)):
        o = outs[2 * j] - lam * outs[2 * j + 1]
        o = o * lax.rsqrt(jnp.mean(o * o, axis=-1, keepdims=True) + NORM_EPS) * sw_ref[...] * (1.0 - lam_init)
        o_ref[0, r0:r0 + ATTN_SUB, :] = o.astype(o_ref.dtype)


def _attn_call(pt, pc, tabs, lam_p, subln_w, lam_init):
    b, lt, _ = pt.shape
    lc = pc.shape[1]
    nh = DIFF_HEADS
    tq = min(lt, ATTN_TQ)
    ct, s1t, s2t = tabs
    tab_q = pl.BlockSpec((tq, 128), lambda bi, h, qi: (qi, 0))
    tab_k = pl.BlockSpec((lt, 128), lambda bi, h, qi: (0, 0))
    return pl.pallas_call(
        functools.partial(_attn_kernel, lt=lt, lc=lc, lam_init=lam_init),
        grid=(b, nh, lt // tq),
        in_specs=[pl.BlockSpec((1, tq, 128), lambda bi, h, qi: (bi, qi, h)),
                  pl.BlockSpec((1, lt, 128), lambda bi, h, qi: (bi, 0, nh + h)),
                  pl.BlockSpec((1, lt, 128), lambda bi, h, qi: (bi, 0, 2 * nh + h)),
                  pl.BlockSpec((1, lc, 128), lambda bi, h, qi: (bi, 0, nh + h)),
                  pl.BlockSpec((1, lc, 128), lambda bi, h, qi: (bi, 0, 2 * nh + h)),
                  tab_q, tab_q, tab_q, tab_k, tab_k, tab_k,
                  pl.BlockSpec((4, DIFF_HD), lambda bi, h, qi: (0, 0)),
                  pl.BlockSpec((1, 128), lambda bi, h, qi: (0, 0))],
        out_specs=pl.BlockSpec((1, tq, 128), lambda bi, h, qi: (bi, qi, h)),
        out_shape=jax.ShapeDtypeStruct((b, lt, nh * DIFF_DV), BF16),
        scratch_shapes=[pltpu.VMEM((2, DIFF_HD, lt + lc), BF16), pltpu.VMEM((lt + lc, 256), BF16)],
        compiler_params=_cparams(("parallel", "parallel", "arbitrary")),
        name="diff_attention",
    )(pt, pt, pt, pc, pc, ct, s1t, s2t, ct, s1t, s2t, lam_p, subln_w.reshape(1, 128))


def _even_weights(w_in, conv_w, a_log, dt_bias, ret_decay, w_out):
    nh, hd = DN_HEADS, HEAD_DIM
    o_z, o_a, o_b = 3 * nh * hd, 4 * nh * hd, 4 * nh * hd + 2 * nh
    o_rq = o_b + 2 * nh

    def col(base, h):
        return w_in[:, base + h * hd:base + (h + 1) * hd]

    main, gate, conv_h, hp = [], [], [], []
    d = w_in.shape[0]
    for h in range(nh):
        main += [col(0, h), col(nh * hd, h), col(2 * nh * hd, h), col(o_z, h),
                 col(o_rq, h), col(o_rq + nh * hd, h), col(o_rq + 2 * nh * hd, h), col(o_rq + 3 * nh * hd, h)]
        gate += [w_in[:, o_a + h:o_a + h + 1], w_in[:, o_a + nh + h:o_a + nh + h + 1],
                 w_in[:, o_b + h:o_b + h + 1], w_in[:, o_b + nh + h:o_b + nh + h + 1],
                 jnp.zeros((d, hd - 4), w_in.dtype)]
        conv_h.append(jnp.concatenate([conv_w[:, h * hd:(h + 1) * hd], conv_w[:, (nh + h) * hd:(nh + h + 1) * hd],
                                       conv_w[:, (2 * nh + h) * hd:(2 * nh + h + 1) * hd]], axis=1))
        ones = jnp.ones((hd,), F32)
        zero = jnp.zeros((hd,), F32)
        hp.append(jnp.stack([ones * a_log[0, h], ones * a_log[1, h]] + [zero] * 6
                            + [ones * dt_bias[0, h], ones * dt_bias[1, h]] + [zero] * 6
                            + [ones * ret_decay[0, h], ones * ret_decay[1, h]] + [zero] * 6))
    w_main = jnp.concatenate(main, axis=1).astype(BF16)
    w_gate = jnp.concatenate(gate, axis=1).astype(BF16)
    w_o = jnp.concatenate([w_out[(j * nh + h) * hd:(j * nh + h + 1) * hd] for h in range(nh) for j in range(2)],
                          axis=0).astype(BF16)
    return w_main, w_gate, jnp.stack(conv_h).astype(F32), jnp.stack(hp).astype(F32), w_o


def _gate_up_weights(w_gate, w_up):
    d, dff = w_gate.shape
    nf = dff // FFN_TF
    both = jnp.concatenate([w_gate.reshape(d, nf, FFN_TF), w_up.reshape(d, nf, FFN_TF)], axis=2)
    return both.reshape(d, nf * 2 * FFN_TF).astype(BF16)


def _ret_tables(l):
    half = HEAD_DIM // 2
    inv = ROPE_BASE ** (-jnp.arange(half, dtype=F32) / half)
    ang = jnp.arange(l).astype(F32)[:, None] * inv[None]
    cs, sn = jnp.cos(ang), jnp.sin(ang)
    return jnp.concatenate([cs, cs], -1), jnp.concatenate([-sn, sn], -1)


def _diff_tables(l):
    n = DIFF_HD // 4
    pos = jnp.arange(l)
    inv = ROPE_BASE ** (-jnp.arange(n, dtype=F32) / n)
    ang = jnp.concatenate([(pos // GRID_W).astype(F32)[:, None] * inv[None],
                           (pos % GRID_W).astype(F32)[:, None] * inv[None]], -1)
    cs, sn, zero = jnp.cos(ang), jnp.sin(ang), jnp.zeros_like(ang)
    c64 = jnp.concatenate([cs, cs], -1)
    s1 = jnp.concatenate([-sn, zero], -1)
    s2 = jnp.concatenate([zero, sn], -1)
    return tuple(jnp.concatenate([t, t], -1) for t in (c64, s1, s2))


def kernel(x, c, ctx, c_ctx, mod_w, mod_b, ln_g, ln_b, e_w_in, e_conv, e_a_log, e_dt_bias, e_norm_w, e_ret_decay,
           e_w_out, o_w_qkv, o_lambda, o_subln_w, o_w_out, f_w_gate, f_w_up, f_conv, f_w_down):
    depth = mod_w.shape[0]
    b, l, d = x.shape
    lc = ctx.shape[1]
    alpha = (2 * depth) ** 0.25
    n_rows = -(-(b + 1) // 8) * 8
    cc = jnp.concatenate([c, c_ctx[None, :], jnp.zeros((n_rows - b - 1, d), F32)], axis=0)
    mod = _mod_call(cc, mod_w, mod_b).reshape(depth, n_rows, 1, 6 * d)[:, :b + 1]
    ret_cos, ret_sin = _ret_tables(l)
    diff_tabs = _diff_tables(l)
    for li in range(depth):
        last = li == depth - 1
        i = li // 2
        mod_l = mod[li]
        if li % 2 == 0:
            w_main, w_gate, conv_h, hp, w_o = _even_weights(e_w_in[i], e_conv[i], e_a_log[i], e_dt_bias[i],
                                                            e_ret_decay[i], e_w_out[i])
            pc, gc = _proj_call(ctx, mod_l, 0, True, [w_main, w_gate], [BF16, F32], "even_proj_ctx")
            pt, gt = _proj_call(x, mod_l, 0, False, [w_main, w_gate], [BF16, F32], "even_proj_lat")
            yc, yt = _even_call(pc, pt, gc, gt, conv_h, hp, e_norm_w[i], ret_cos, ret_sin)
        else:
            lam_init = 0.8 - 0.6 * math.exp(-0.3 * li)
            w_qkv = o_w_qkv[i].astype(BF16)
            (pc,) = _proj_call(ctx, mod_l, 0, True, [w_qkv], [BF16], "odd_proj_ctx")
            (pt,) = _proj_call(x, mod_l, 0, False, [w_qkv], [BF16], "odd_proj_lat")
            yt = _attn_call(pt, pc, diff_tabs, o_lambda[i], o_subln_w[i], lam_init)
            w_o = o_w_out[i].astype(BF16)
            yc = None
            if not last:
                raise NotImplementedError("context update after a differential-attention layer")
        wgu, wd = _gate_up_weights(f_w_gate[li], f_w_up[li]), f_w_down[li].astype(BF16)
        wcv = f_conv[li].reshape(9, -1)
        x1, hf = _outln_call(yt, x, mod_l, 2, False, w_o, ln_g[li, 0], ln_b[li, 0], alpha, 3, "mix_out_lat")
        x = _ffn_call(hf, wgu, wcv, wd, GRID_W, True, x1, mod_l, 5, False, ln_g[li, 1], ln_b[li, 1], alpha, "ffn_lat")
        if not last:
            c1, hcf = _outln_call(yc, ctx, mod_l, 2, True, w_o, ln_g[li, 0], ln_b[li, 0], alpha, 3, "mix_out_ctx")
            grp = max(g for g in range(1, b + 1) if b % g == 0 and g * lc <= l)
            ctx = _ffn_call(hcf.reshape(b // grp, grp * lc, d), wgu, wcv, wd, lc, False,
                            c1.reshape(b // grp, grp * lc, d), mod_l, 5, True, ln_g[li, 1], ln_b[li, 1], alpha,
                            "ffn_ctx").reshape(b, lc, d)
    return x
```

```python
import functools
import math

import jax
import jax.numpy as jnp
from jax import lax
from jax.experimental import pallas as pl
from jax.experimental.pallas import tpu as pltpu

F32 = jnp.float32
BF16 = jnp.bfloat16

GRID_W = 64
DN_HEADS = 4
RET_HEADS = 4
HEAD_DIM = 128
DIFF_HEADS = 8
DIFF_HD = 64
DIFF_DV = 2 * DIFF_HD
ROPE_BASE = 10000.0
LN_EPS = 1e-5
NORM_EPS = 1e-6

CHUNK = 128
INV_BASE = 16
PAR_CHUNKS = 4
PROJ_ROWS = 1024
ATTN_TQ = 2048
ATTN_SUB = 128
SCORE_AHEAD = 3
LOG2_E = 1.4426950408889634
GATE_AHEAD = 2
FFN_ROWS = 512
FFN_LEAD_EDGES = (128, 256)
FFN_TF = 256
FFN_SUB = 64
BF16_ROWS = 16
VMEM_LIMIT = 56 * 1024 * 1024


def _dot(a, b):
    return jnp.dot(a, b, preferred_element_type=F32)


def _silu(x):
    return x * jax.nn.sigmoid(x)


def _softplus(x):
    return jnp.maximum(x, 0.0) + jnp.log1p(jnp.exp(-jnp.abs(x)))


def _cparams(sem):
    return pltpu.CompilerParams(dimension_semantics=sem, vmem_limit_bytes=VMEM_LIMIT)


def _mod_kernel(cc_ref, w_ref, b_ref, o_ref):
    a = _silu(cc_ref[...])
    w = w_ref[0]
    a_hi = a.astype(BF16)
    a_lo = (a - a_hi.astype(F32)).astype(BF16)
    w_hi = w.astype(BF16)
    w_lo = (w - w_hi.astype(F32)).astype(BF16)
    o_ref[0] = _dot(a_hi, w_hi) + _dot(a_hi, w_lo) + _dot(a_lo, w_hi) + b_ref[0]


def _mod_call(cc, mod_w, mod_b):
    depth, d, n = mod_w.shape
    r = cc.shape[0]
    tn = min(n, 1536)
    return pl.pallas_call(
        _mod_kernel,
        grid=(depth, n // tn),
        in_specs=[pl.BlockSpec((r, d), lambda l, j: (0, 0)),
                  pl.BlockSpec((1, d, tn), lambda l, j: (l, 0, j)),
                  pl.BlockSpec((1, 1, tn), lambda l, j: (l, 0, j))],
        out_specs=pl.BlockSpec((1, r, tn), lambda l, j: (l, 0, j)),
        out_shape=jax.ShapeDtypeStruct((depth, r, n), F32),
        compiler_params=_cparams(("parallel", "parallel")),
        name="adaln_maps",
    )(cc, mod_w, mod_b.reshape(depth, 1, n))


def _mod_spec(d, k, is_ctx, nb, ngrid):
    if ngrid == 2:
        imap = (lambda b, i: (nb, 0, k)) if is_ctx else (lambda b, i: (b, 0, k))
    else:
        imap = (lambda b, i, j: (nb, 0, k)) if is_ctx else (lambda b, i, j: (b, 0, k))
    return pl.BlockSpec((1, 1, d), imap)


def _proj_kernel(x_ref, sh_ref, sc_ref, *refs, n_w, tn):
    w_refs, o_refs = refs[:n_w], refs[n_w:]
    h = (x_ref[0] * (1.0 + sc_ref[0]) + sh_ref[0]).astype(BF16)
    for w_ref, o_ref in zip(w_refs, o_refs):
        n = w_ref.shape[1]
        for n0 in range(0, n, tn):
            n1 = min(n0 + tn, n)
            o_ref[0, :, n0:n1] = _dot(h, w_ref[:, n0:n1]).astype(o_ref.dtype)


def _proj_call(x, mod_l, k_shift, is_ctx, weights, out_dtypes, name):
    b, l, d = x.shape
    nb = mod_l.shape[0] - 1
    tm = min(l, PROJ_ROWS)
    in_specs = [pl.BlockSpec((1, tm, d), lambda bi, i: (bi, i, 0)),
                _mod_spec(d, k_shift, is_ctx, nb, 2),
                _mod_spec(d, k_shift + 1, is_ctx, nb, 2)]
    out_specs, out_shape = [], []
    for w, dt in zip(weights, out_dtypes):
        n = w.shape[1]
        in_specs.append(pl.BlockSpec((d, n), lambda bi, i: (0, 0)))
        out_specs.append(pl.BlockSpec((1, tm, n), lambda bi, i: (bi, i, 0)))
        out_shape.append(jax.ShapeDtypeStruct((b, l, n), dt))
    return pl.pallas_call(
        functools.partial(_proj_kernel, n_w=len(weights), tn=512),
        grid=(b, l // tm),
        in_specs=in_specs, out_specs=out_specs, out_shape=out_shape,
        compiler_params=_cparams(("parallel", "parallel")),
        name=name,
    )(x, mod_l, mod_l, *weights)


def _outln_kernel(*refs, alpha, has_w, has_h):
    refs = list(refs)
    y_ref, x_ref, g_ref = refs[:3]
    refs = refs[3:]
    w_ref = refs.pop(0) if has_w else None
    lng_ref, lnb_ref = refs[:2]
    refs = refs[2:]
    if has_h:
        sh_ref, sc_ref, o_ref, h_ref = refs
    else:
        (o_ref,) = refs
    t = _dot(y_ref[0], w_ref[...]) if has_w else y_ref[0].astype(F32)
    z = alpha * x_ref[0] + g_ref[0] * t
    mu = jnp.mean(z, axis=-1, keepdims=True)
    zc = z - mu
    var = jnp.mean(zc * zc, axis=-1, keepdims=True)
    o = zc * lax.rsqrt(var + LN_EPS) * lng_ref[...] + lnb_ref[...]
    o_ref[0] = o
    if has_h:
        h_ref[0] = (o * (1.0 + sc_ref[0]) + sh_ref[0]).astype(BF16)


def _outln_call(y, x, mod_l, k_gate, is_ctx, w, ln_g, ln_b, alpha, k_shift_next, name):
    b, l, d = x.shape
    kdim = y.shape[-1]
    nb = mod_l.shape[0] - 1
    tm = min(l, 512)
    has_w, has_h = w is not None, k_shift_next is not None
    args = [y, x, mod_l]
    in_specs = [pl.BlockSpec((1, tm, kdim), lambda bi, i: (bi, i, 0)),
                pl.BlockSpec((1, tm, d), lambda bi, i: (bi, i, 0)),
                _mod_spec(d, k_gate, is_ctx, nb, 2)]
    if has_w:
        args.append(w)
        in_specs.append(pl.BlockSpec((kdim, d), lambda bi, i: (0, 0)))
    args += [ln_g.reshape(1, d), ln_b.reshape(1, d)]
    in_specs += [pl.BlockSpec((1, d), lambda bi, i: (0, 0))] * 2
    out_specs = [pl.BlockSpec((1, tm, d), lambda bi, i: (bi, i, 0))]
    out_shape = [jax.ShapeDtypeStruct((b, l, d), F32)]
    if has_h:
        args += [mod_l, mod_l]
        in_specs += [_mod_spec(d, k_shift_next, is_ctx, nb, 2), _mod_spec(d, k_shift_next + 1, is_ctx, nb, 2)]
        out_specs.append(pl.BlockSpec((1, tm, d), lambda bi, i: (bi, i, 0)))
        out_shape.append(jax.ShapeDtypeStruct((b, l, d), BF16))
    res = pl.pallas_call(
        functools.partial(_outln_kernel, alpha=alpha, has_w=has_w, has_h=has_h),
        grid=(b, l // tm),
        in_specs=in_specs, out_specs=out_specs, out_shape=out_shape,
        compiler_params=_cparams(("parallel", "parallel")),
        name=name,
    )(*args)
    return res if has_h else res[0]


def _ffn_kernel(h_ref, wgu_ref, wc_ref, wd_ref, x_ref, g_ref, lng_ref, lnb_ref, o_ref, apad, ubuf,
                *, l, w, vertical, pad, rb, alpha):
    f = pl.program_id(1)
    tf = wd_ref.shape[0]

    @pl.when(f == 0)
    def _():
        o_ref[0] = jnp.zeros(o_ref.shape[1:], F32)
        apad[0:pad, :] = jnp.zeros((pad, tf), F32)
        apad[pad + l:pad + l + pad, :] = jnp.zeros((pad, tf), F32)

    edges = [e for e in FFN_LEAD_EDGES if e < rb and e < l] + list(range(rb, l, rb)) + [l] if l > rb else [l]
    blocks = list(zip([0] + edges[:-1], edges))
    nblk = len(blocks)
    dis = (-1, 0, 1) if vertical else (0,)
    wrow = [wc_ref[k:k + 1, :] for k in range(9)]
    row8 = lax.broadcasted_iota(jnp.int32, (8, tf), 0)

    def gate(i):
        lo, hi = blocks[i]
        gu = _dot(h_ref[0, lo:hi, :], wgu_ref[...])
        apad[pad + lo:pad + hi, :] = gu[:, 0:tf]
        ubuf[lo:hi, :] = gu[:, tf:2 * tf]

    def colsum(dj, lo, n):
        tot = None
        for di in dis:
            term = apad[lo + di * w:lo + di * w + n, :] * wrow[(di + 1) * 3 + dj + 1]
            tot = term if tot is None else tot + term
        return tot

    def down(i, t):
        lo, hi = blocks[i]
        o_ref[0, lo:hi, :] += _dot(t, wd_ref[...])

    for i in range(min(GATE_AHEAD, nblk)):
        gate(i)
    t_prev = None
    for i in range(nblk):
        if i + GATE_AHEAD < nblk:
            gate(i + GATE_AHEAD)
        r0, r1 = blocks[i]
        base = pad + r0
        if t_prev is not None:
            down(i - 1, t_prev)
        ts = []
        for s0 in range(0, r1 - r0, FFN_SUB):
            lo = base + s0
            n_ext = FFN_SUB + 16
            left = pltpu.roll(colsum(-1, lo - 8, n_ext), 1, 0)[8:8 + FFN_SUB, :]
            right = pltpu.roll(colsum(1, lo - 8, n_ext), n_ext - 1, 0)[8:8 + FFN_SUB, :]
            if (r0 + s0) % w == 0:
                left = jnp.concatenate([jnp.where(row8 == 0, 0.0, left[0:8, :]), left[8:, :]], axis=0)
            if (r0 + s0 + FFN_SUB) % w == 0:
                right = jnp.concatenate([right[:FFN_SUB - 8, :], jnp.where(row8 == 7, 0.0, right[FFN_SUB - 8:, :])],
                                        axis=0)
            conv = colsum(0, lo, FFN_SUB) + left + right
            ts.append((_silu(conv) * ubuf[r0 + s0:r0 + s0 + FFN_SUB, :]).astype(BF16))
        t_prev = jnp.concatenate(ts, axis=0)
    down(nblk - 1, t_prev)

    @pl.when(f == pl.num_programs(1) - 1)
    def _():
        for lo in range(0, l, FFN_SUB):
            z = alpha * x_ref[0, lo:lo + FFN_SUB, :] + g_ref[0] * o_ref[0, lo:lo + FFN_SUB, :]
            mu = jnp.mean(z, axis=-1, keepdims=True)
            zc = z - mu
            var = jnp.mean(zc * zc, axis=-1, keepdims=True)
            o_ref[0, lo:lo + FFN_SUB, :] = zc * lax.rsqrt(var + LN_EPS) * lng_ref[...] + lnb_ref[...]


def _ffn_call(hf, w_gate_up, w_conv, w_down, grid_w, vertical, x, mod_l, k_gate, is_ctx, ln_g, ln_b, alpha, name):
    b, l, d = hf.shape
    dff = w_down.shape[0]
    tf = FFN_TF
    rb = min(l, FFN_ROWS)
    pad = grid_w + 8 if vertical else 8
    assert grid_w % FFN_SUB == 0 and l % rb == 0 and rb % FFN_SUB == 0 and all(e % FFN_SUB == 0 for e in FFN_LEAD_EDGES)
    return pl.pallas_call(
        functools.partial(_ffn_kernel, l=l, w=grid_w, vertical=vertical, pad=pad, rb=rb, alpha=alpha),
        grid=(b, dff // tf),
        in_specs=[pl.BlockSpec((1, l, d), lambda bi, f: (bi, 0, 0)),
                  pl.BlockSpec((d, 2 * tf), lambda bi, f: (0, f)),
                  pl.BlockSpec((9, tf), lambda bi, f: (0, f)),
                  pl.BlockSpec((tf, d), lambda bi, f: (f, 0)),
                  pl.BlockSpec((1, l, d), lambda bi, f: (bi, 0, 0)),
                  _mod_spec(d, k_gate, is_ctx, mod_l.shape[0] - 1, 2),
                  pl.BlockSpec((1, d), lambda bi, f: (0, 0)),
                  pl.BlockSpec((1, d), lambda bi, f: (0, 0))],
        out_specs=pl.BlockSpec((1, l, d), lambda bi, f: (bi, 0, 0)),
        out_shape=jax.ShapeDtypeStruct((b, l, d), F32),
        scratch_shapes=[pltpu.VMEM((l + 2 * pad, tf), F32), pltpu.VMEM((l, tf), F32)],
        compiler_params=_cparams(("parallel", "arbitrary")),
        name=name,
    )(hf, w_gate_up, w_conv, w_down, x, mod_l, ln_g.reshape(1, d), ln_b.reshape(1, d))


def _even_kernel(pc_ref, pt_ref, gc_ref, gt_ref, cw_ref, hp_ref, nw_ref, cos_ref, sin_ref,
                 yc_ref, yt_ref,
                 u_s, wq_s, qkk_s, cd_s, ri_s, kvf_s, kvb_s, rqf_s, rqb_s, of_s, ob_s, rb_s, *, nc, nt):
    c = CHUNK
    hd = HEAD_DIM
    ntot = nc + nt
    ri = lax.broadcasted_iota(jnp.int32, (c, c), 0)
    ci = lax.broadcasted_iota(jnp.int32, (c, c), 1)
    rif = ri.astype(F32)
    cif = ci.astype(F32)
    eye = (ri == ci).astype(F32)

    def same_block(size):
        sh = size.bit_length() - 1
        return jnp.right_shift(ri, sh) == jnp.right_shift(ci, sh)

    diag_mask = same_block(INV_BASE).astype(F32)
    off_masks = []
    size = INV_BASE
    while size < c:
        off_masks.append(jnp.where(same_block(2 * size), 1.0, 0.0) - jnp.where(same_block(size), 1.0, 0.0))
        size *= 2
    hp = hp_ref[0]
    a_log8, dt8 = hp[0:8, :], hp[8:16, :]
    lg_f = -jnp.exp(hp[16:17, :])
    lg_b = -jnp.exp(hp[17:18, :])
    dec_bi = (jnp.where(ri >= ci, jnp.exp(lg_f * (rif - cif)), 0.0)
              + jnp.where(ci >= ri, jnp.exp(lg_b * (cif - rif)), 0.0))
    gq_f = jnp.exp(lg_f * (rif + 1.0))
    gq_b = jnp.exp(lg_b * (c - rif))
    gk_f = jnp.exp(lg_f * (c - 1.0 - cif))
    gk_b = jnp.exp(lg_b * cif)
    cd_f = jnp.exp(lg_f * c)
    cd_b = jnp.exp(lg_b * c)
    row8 = lax.broadcasted_iota(jnp.int32, (8, c), 0)
    lane8 = lax.broadcasted_iota(jnp.int32, (8, c), 1)

    def gate_rows(g):
        gt8 = g.T[0:8, :]
        la = -jnp.exp(a_log8) * _softplus(gt8 + dt8)
        pre, suf = la, la
        s = 1
        while s < c:
            pre = pre + jnp.where(lane8 >= s, pltpu.roll(pre, s, 1), 0.0)
            suf = suf + jnp.where(lane8 < c - s, pltpu.roll(suf, c - s, 1), 0.0)
            s *= 2
        cum = jnp.where(row8 == 0, pre, suf)
        ecum = pltpu.roll(jnp.exp(cum), 4, 0)
        return jnp.where(row8 < 2, cum, jnp.where(row8 < 4, jax.nn.sigmoid(gt8), jnp.where(row8 < 6, ecum, 0.0)))

    def chunk_prep(src_ref, gsrc_ref, m, ns, off, rot):
        ln = ns * c
        r0 = pl.multiple_of(m * c, c)
        rows = pl.ds(r0, c)
        x = src_ref[0, rows, 0:3 * hd].astype(F32)
        pstart = pl.multiple_of(jnp.maximum(r0 - BF16_ROWS, 0), BF16_ROWS)
        nstart = pl.multiple_of(jnp.minimum(r0 + c, ln - BF16_ROWS), BF16_ROWS)
        prev = src_ref[0, pl.ds(pstart, BF16_ROWS), 0:3 * hd].astype(F32)[BF16_ROWS - 1:BF16_ROWS, :]
        nxt = src_ref[0, pl.ds(nstart, BF16_ROWS), 0:3 * hd].astype(F32)[0:1, :]
        prev = prev * jnp.where(m > 0, 1.0, 0.0)
        nxt = nxt * jnp.where(m < ns - 1, 1.0, 0.0)
        rr = lax.broadcasted_iota(jnp.int32, (c, 3 * hd), 0)
        xm1 = jnp.where(rr == 0, prev, pltpu.roll(x, 1, 0))
        xp1 = jnp.where(rr == c - 1, nxt, pltpu.roll(x, c - 1, 0))
        cw = cw_ref[0]
        y = _silu(xm1 * cw[0:1, :] + x * cw[1:2, :] + xp1 * cw[2:3, :])
        q = y[:, 0:hd]
        k = y[:, hd:2 * hd]
        q = q * (lax.rsqrt(jnp.sum(q * q, axis=-1, keepdims=True) + NORM_EPS) * (hd ** -0.5))
        k = k * lax.rsqrt(jnp.sum(k * k, axis=-1, keepdims=True) + NORM_EPS)
        xt8 = gate_rows(gsrc_ref[0, rows, :])
        xg = jnp.concatenate([xt8, jnp.zeros((c - 8, c), F32)], axis=0).T
        rq = src_ref[0, rows, 4 * hd:5 * hd].astype(F32)
        rk = src_ref[0, rows, 5 * hd:6 * hd].astype(F32)
        if rot:
            cs, sn = cos_ref[rows, :], sin_ref[rows, :]
            rq = rq * cs + pltpu.roll(rq, hd // 2, 1) * sn
            rk = rk * cs + pltpu.roll(rk, hd // 2, 1) * sn
        return dict(n=m + off, q=q, k=k, kt=k.T, v=y[:, 2 * hd:3 * hd], xt8=xt8, xg=xg,
                    rq=rq, rkt=(rk * (hd ** -0.5)).T, rv=src_ref[0, rows, 6 * hd:7 * hd])

    def chunks_work(preps):
        for pr in preps:
            pr["kt16"] = pr["kt"].astype(BF16)
        kks = [_dot(pr["k"].astype(BF16), pr["kt16"]) for pr in preps]
        qks = [_dot(pr["q"].astype(BF16), pr["kt16"]) for pr in preps]
        scs = [_dot(pr["rq"].astype(BF16), pr["rkt"].astype(BF16)) for pr in preps]
        chains = []
        for pr, kk, qk in zip(preps, kks, qks):
            for d in range(2):
                xg, xt8 = pr["xg"], pr["xt8"]
                g, beta, eg = xg[:, d:d + 1], xg[:, 2 + d:3 + d], xg[:, 4 + d:5 + d]
                g_row = xt8[d:d + 1, :]
                last = 0 if d else c - 1
                g_last = xt8[d:d + 1, last:last + 1]
                e = jnp.exp(jnp.minimum(g - g_row, 0.0))
                incl = (ri <= ci) if d else (ri >= ci)
                strict = (ri < ci) if d else (ri > ci)
                nm = -(beta * kk) * jnp.where(strict, e, 0.0)
                nd = nm * diag_mask
                chains.append(dict(pr=pr, d=d, nm=nm, nd16=nd.astype(BF16), p=eye + nd, beta=beta, eg=eg,
                                   qkm=qk * jnp.where(incl, e, 0.0), kgt=pr["kt"] * jnp.exp(g_last - g_row),
                                   cd=jnp.broadcast_to(jnp.exp(g_last), (8, c))))
        for ch in chains:
            ch["npow"] = _dot(ch["nd16"], ch["nd16"])
        lvl = 2
        while lvl * 2 < INV_BASE:
            rs = [_dot(jnp.concatenate([ch["p"], ch["npow"]], axis=0).astype(BF16), ch["npow"].astype(BF16))
                  for ch in chains]
            for ch, r in zip(chains, rs):
                ch["p"] = ch["p"] + r[0:c, :]
                ch["npow"] = r[c:, :]
            lvl *= 2
        rs = [_dot(ch["p"].astype(BF16), ch["npow"].astype(BF16)) for ch in chains]
        for ch, r in zip(chains, rs):
            ch["p"] = ch["p"] + r
        for om in off_masks:
            xs = [_dot(ch["p"].astype(BF16), (ch["nm"] * om).astype(BF16)) for ch in chains]
            rs = [_dot(x.astype(BF16), ch["p"].astype(BF16)) for ch, x in zip(chains, xs)]
            for ch, r in zip(chains, rs):
                ch["p"] = ch["p"] + r
        sols = [_dot(ch["p"].astype(BF16),
                     jnp.concatenate([ch["pr"]["v"] * ch["beta"], ch["pr"]["k"] * (ch["beta"] * ch["eg"])],
                                     axis=1).astype(BF16)) for ch in chains]
        rets = [_dot(jnp.concatenate([sc * dec_bi, pr["rkt"] * gk_f, pr["rkt"] * gk_b], axis=0).astype(BF16), pr["rv"])
                for pr, sc in zip(preps, scs)]
        for ch, sol in zip(chains, sols):
            idx = ch["d"] * ntot + ch["pr"]["n"]
            u_s[idx] = sol[:, 0:hd]
            wq_s[idx] = jnp.concatenate([sol[:, hd:2 * hd], ch["pr"]["q"] * ch["eg"]], axis=0).astype(BF16)
            qkk_s[idx] = jnp.concatenate([ch["qkm"], ch["kgt"]], axis=0).astype(BF16)
            cd_s[idx] = ch["cd"]
        for pr, r in zip(preps, rets):
            n = pr["n"]
            ri_s[n] = r[0:c, :]
            kvf_s[n] = r[c:2 * c, :]
            kvb_s[n] = r[2 * c:3 * c, :]
            rqf_s[n] = (pr["rq"] * gq_f).astype(BF16)
            rqb_s[n] = (pr["rq"] * gq_b).astype(BF16)

    def segment(src_ref, gsrc_ref, ns, off, rot):
        par = math.gcd(ns, PAR_CHUNKS)

        def body(j, carry):
            chunks_work([chunk_prep(src_ref, gsrc_ref, j * par + t, ns, off, rot) for t in range(par)])
            return carry

        lax.fori_loop(0, ns // par, body, 0)

    segment(pc_ref, gc_ref, nc, 0, False)
    segment(pt_ref, gt_ref, nt, nc, True)

    def emit(src_ref, dst_ref, m, o, r):
        rows = pl.ds(pl.multiple_of(m * c, c), c)
        zg = src_ref[0, rows, 3 * hd:4 * hd].astype(F32)
        dn = o * lax.rsqrt(jnp.mean(o * o, axis=-1, keepdims=True) + NORM_EPS) * nw_ref[...] * _silu(zg)
        mu = jnp.mean(r, axis=-1, keepdims=True)
        rc = r - mu
        rg = src_ref[0, rows, 7 * hd:8 * hd].astype(F32)
        rt = rc * lax.rsqrt(jnp.mean(rc * rc, axis=-1, keepdims=True) + NORM_EPS) * _silu(rg)
        dst_ref[0, rows, 0:hd] = dn.astype(dst_ref.dtype)
        dst_ref[0, rows, hd:2 * hd] = rt.astype(dst_ref.dtype)

    def step(i, carry, tail):
        s_f, s_b, t_f, t_b = carry
        nf = i
        nbk = jnp.where(i < nc, nc - 1 - i, ntot - 1 - (i - nc))
        idxs = (nf, ntot + nbk)
        r1s = [_dot(wq_s[idx], st.astype(BF16)) for idx, st in zip(idxs, (s_f, s_b))]
        r_f = ri_s[nf] + _dot(rqf_s[nf], t_f.astype(BF16))
        r_b = _dot(rqb_s[nbk], t_b.astype(BF16))
        r2s = [_dot(qkk_s[idx], (u_s[idx] - r1[0:c, :]).astype(BF16)) for idx, r1 in zip(idxs, r1s)]
        o_f, o_b = [r1[c:, :] + r2[0:c, :] for r1, r2 in zip(r1s, r2s)]
        s_f, s_b = [cd_s[idx][0:1, :] * st + r2[c:, :] for idx, st, r2 in zip(idxs, (s_f, s_b), r2s)]
        t_f = cd_f * t_f + kvf_s[nf]
        t_b = cd_b * t_b + kvb_s[nbk]
        if tail:
            other_b, other_rb, other_f, other_rf = ob_s[nf], rb_s[nf], of_s[nbk], ri_s[nbk]
            emit(pt_ref, yt_ref, nf - nc, o_f + other_b, r_f + other_rb)
            emit(pt_ref, yt_ref, nbk - nc, other_f + o_b, other_rf + r_b)
        else:
            of_s[nf] = o_f
            ob_s[nbk] = o_b
            ri_s[nf] = r_f
            rb_s[nbk] = r_b
        return s_f, s_b, t_f, t_b

    n_head = nc + nt // 2
    z = jnp.zeros((hd, hd), F32)
    carry = lax.fori_loop(0, n_head, functools.partial(step, tail=False), (z, z, z, z))
    lax.fori_loop(n_head, ntot, functools.partial(step, tail=True), carry)

    def ctx_finish(m, carry):
        emit(pc_ref, yc_ref, m, of_s[m] + ob_s[m], ri_s[m] + rb_s[m])
        return carry

    lax.fori_loop(0, nc, ctx_finish, 0)


def _even_call(pc, pt, gc, gt, conv_h, hp, norm_w, cos_t, sin_t):
    b, lc, _ = pc.shape
    lt = pt.shape[1]
    nc, nt = lc // CHUNK, lt // CHUNK
    assert lc % CHUNK == 0 and lt % (2 * CHUNK) == 0
    ntot = nc + nt
    nh = DN_HEADS
    hd = HEAD_DIM
    blk = 8 * hd
    f32_scr = pltpu.VMEM((ntot, CHUNK, CHUNK), F32)
    bf16_scr = pltpu.VMEM((ntot, CHUNK, CHUNK), BF16)
    return pl.pallas_call(
        functools.partial(_even_kernel, nc=nc, nt=nt),
        grid=(b, nh),
        in_specs=[pl.BlockSpec((1, lc, blk), lambda bi, h: (bi, 0, h)),
                  pl.BlockSpec((1, lt, blk), lambda bi, h: (bi, 0, h)),
                  pl.BlockSpec((1, lc, hd), lambda bi, h: (bi, 0, h)),
                  pl.BlockSpec((1, lt, hd), lambda bi, h: (bi, 0, h)),
                  pl.BlockSpec((1, 3, 3 * hd), lambda bi, h: (h, 0, 0)),
                  pl.BlockSpec((1, 24, hd), lambda bi, h: (h, 0, 0)),
                  pl.BlockSpec((1, hd), lambda bi, h: (0, 0)),
                  pl.BlockSpec((lt, hd), lambda bi, h: (0, 0)),
                  pl.BlockSpec((lt, hd), lambda bi, h: (0, 0))],
        out_specs=[pl.BlockSpec((1, lc, 2 * hd), lambda bi, h: (bi, 0, h)),
                   pl.BlockSpec((1, lt, 2 * hd), lambda bi, h: (bi, 0, h))],
        out_shape=[jax.ShapeDtypeStruct((b, lc, nh * 2 * hd), BF16),
                   jax.ShapeDtypeStruct((b, lt, nh * 2 * hd), BF16)],
        scratch_shapes=[pltpu.VMEM((2 * ntot, CHUNK, CHUNK), F32),
                        pltpu.VMEM((2 * ntot, 2 * CHUNK, CHUNK), BF16),
                        pltpu.VMEM((2 * ntot, 2 * CHUNK, CHUNK), BF16),
                        pltpu.VMEM((2 * ntot, 8, CHUNK), F32),
                        f32_scr, f32_scr, f32_scr,
                        bf16_scr, bf16_scr,
                        f32_scr, f32_scr, f32_scr],
        compiler_params=_cparams(("parallel", "arbitrary")),
        name="deltanet_retention",
    )(pc, pt, gc, gt, conv_h, hp, norm_w.reshape(1, hd), cos_t, sin_t)


def _rot2d(x, c_ref, s1_ref, s2_ref):
    return x * c_ref[...] + pltpu.roll(x, 96, 1) * s1_ref[...] + pltpu.roll(x, 32, 1) * s2_ref[...]


def _attn_kernel(q_ref, kt_ref, vt_ref, kc_ref, vc_ref, qc_ref, qs1_ref, qs2_ref, kc_t, ks1_t, ks2_t,
                 lam_ref, sw_ref, o_ref, kfull, vfull, *, lt, lc, lam_init):
    qi = pl.program_id(2)
    hd = DIFF_HD
    lk = lt + lc

    @pl.when(qi == 0)
    def _():
        for r0 in range(0, lt, 128):
            kr = _rot2d(kt_ref[0, r0:r0 + 128, :].astype(F32), kc_t.at[r0:r0 + 128, :],
                        ks1_t.at[r0:r0 + 128, :], ks2_t.at[r0:r0 + 128, :]).T
            kfull[0, :, r0:r0 + 128] = kr[0:hd, :].astype(BF16)
            kfull[1, :, r0:r0 + 128] = kr[hd:2 * hd, :].astype(BF16)
        for r0 in range(0, lc, 128):
            kr = kc_ref[0, r0:r0 + 128, :].astype(F32).T
            kfull[0, :, lt + r0:lt + r0 + 128] = kr[0:hd, :].astype(BF16)
            kfull[1, :, lt + r0:lt + r0 + 128] = kr[hd:2 * hd, :].astype(BF16)
        lane = lax.broadcasted_iota(jnp.int32, (lk, 128), 1)
        vfull[0:lt, 0:128] = vt_ref[0].astype(vfull.dtype)
        vfull[lt:lk, 0:128] = vc_ref[0].astype(vfull.dtype)
        vfull[:, 128:256] = jnp.where(lane == 0, 1.0, 0.0).astype(BF16)

    lp = lam_ref[...]
    lam = (jnp.exp(jnp.sum(lp[0:1, :] * lp[1:2, :], axis=-1, keepdims=True))
           - jnp.exp(jnp.sum(lp[2:3, :] * lp[3:4, :], axis=-1, keepdims=True)) + lam_init)
    q = (_rot2d(q_ref[0].astype(F32), qc_ref, qs1_ref, qs2_ref) * (hd ** -0.5 * LOG2_E)).astype(BF16)
    tq = q.shape[0]
    units = [(r0, comp) for r0 in range(0, tq, ATTN_SUB) for comp in range(2)]

    def score(r0, comp):
        return _dot(q[r0:r0 + ATTN_SUB, comp * hd:(comp + 1) * hd], kfull[comp])

    def weighted(s):
        p = jnp.exp2(s - jnp.max(s, axis=-1, keepdims=True)).astype(BF16)
        r = _dot(p, vfull[...])
        return r[:, 0:128] / r[:, 128:129]

    scores = [score(*u) for u in units[:SCORE_AHEAD]]
    outs = []
    for i in range(len(units)):
        if i + SCORE_AHEAD < len(units):
            scores.append(score(*units[i + SCORE_AHEAD]))
        outs.append(weighted(scores[i]))
        scores[i] = None
    for j, r0 in enumerate(range(0, tq, ATTN_SUB)):
        o = outs[2 * j] - lam * outs[2 * j + 1]
        o = o * lax.rsqrt(jnp.mean(o * o, axis=-1, keepdims=True) + NORM_EPS) * sw_ref[...] * (1.0 - lam_init)
        o_ref[0, r0:r0 + ATTN_SUB, :] = o.astype(o_ref.dtype)


def _attn_call(pt, pc, tabs, lam_p, subln_w, lam_init):
    b, lt, _ = pt.shape
    lc = pc.shape[1]
    nh = DIFF_HEADS
    tq = min(lt, ATTN_TQ)
    ct, s1t, s2t = tabs
    tab_q = pl.BlockSpec((tq, 128), lambda bi, h, qi: (qi, 0))
    tab_k = pl.BlockSpec((lt, 128), lambda bi, h, qi: (0, 0))
    return pl.pallas_call(
        functools.partial(_attn_kernel, lt=lt, lc=lc, lam_init=lam_init),
        grid=(b, nh, lt // tq),
        in_specs=[pl.BlockSpec((1, tq, 128), lambda bi, h, qi: (bi, qi, h)),
                  pl.BlockSpec((1, lt, 128), lambda bi, h, qi: (bi, 0, nh + h)),
                  pl.BlockSpec((1, lt, 128), lambda bi, h, qi: (bi, 0, 2 * nh + h)),
                  pl.BlockSpec((1, lc, 128), lambda bi, h, qi: (bi, 0, nh + h)),
                  pl.BlockSpec((1, lc, 128), lambda bi, h, qi: (bi, 0, 2 * nh + h)),
                  tab_q, tab_q, tab_q, tab_k, tab_k, tab_k,
                  pl.BlockSpec((4, DIFF_HD), lambda bi, h, qi: (0, 0)),
                  pl.BlockSpec((1, 128), lambda bi, h, qi: (0, 0))],
        out_specs=pl.BlockSpec((1, tq, 128), lambda bi, h, qi: (bi, qi, h)),
        out_shape=jax.ShapeDtypeStruct((b, lt, nh * DIFF_DV), BF16),
        scratch_shapes=[pltpu.VMEM((2, DIFF_HD, lt + lc), BF16), pltpu.VMEM((lt + lc, 256), BF16)],
        compiler_params=_cparams(("parallel", "parallel", "arbitrary")),
        name="diff_attention",
    )(pt, pt, pt, pc, pc, ct, s1t, s2t, ct, s1t, s2t, lam_p, subln_w.reshape(1, 128))


def _even_weights(w_in, conv_w, a_log, dt_bias, ret_decay, w_out):
    nh, hd = DN_HEADS, HEAD_DIM
    d = w_in.shape[0]
    o_z, o_a, o_b = 3 * nh * hd, 4 * nh * hd, 4 * nh * hd + 2 * nh
    o_rq = o_b + 2 * nh
    parts = jnp.concatenate([w_in[:, 0:o_z].reshape(d, 3, nh, hd), w_in[:, o_z:o_a].reshape(d, 1, nh, hd),
                             w_in[:, o_rq:].reshape(d, 4, nh, hd)], axis=1)
    w_main = parts.transpose(0, 2, 1, 3).reshape(d, nh * 8 * hd).astype(BF16)
    gates = jnp.stack([w_in[:, o_a:o_b].reshape(d, 2, nh), w_in[:, o_b:o_rq].reshape(d, 2, nh)], axis=1)
    gates = gates.reshape(d, 4, nh).transpose(0, 2, 1)
    w_gate = jnp.pad(gates, ((0, 0), (0, 0), (0, hd - 4))).reshape(d, nh * hd).astype(BF16)
    conv_h = conv_w.reshape(3, 3, nh, hd).transpose(2, 0, 1, 3).reshape(nh, 3, 3 * hd).astype(F32)
    zeros = jnp.zeros((nh, 6), F32)
    rows = jnp.concatenate([a_log.T, zeros, dt_bias.T, zeros, ret_decay.T, zeros], axis=1)
    hp = jnp.broadcast_to(rows[:, :, None], (nh, 24, hd)).astype(F32)
    w_o = w_out.reshape(2, nh, hd, -1).transpose(1, 0, 2, 3).reshape(2 * nh * hd, -1).astype(BF16)
    return w_main, w_gate, conv_h, hp, w_o


def _gate_up_weights(w_gate, w_up):
    d, dff = w_gate.shape
    nf = dff // FFN_TF
    both = jnp.concatenate([w_gate.astype(BF16).reshape(d, nf, FFN_TF), w_up.astype(BF16).reshape(d, nf, FFN_TF)],
                           axis=2)
    return both.reshape(d, nf * 2 * FFN_TF)


def _ret_tables(l):
    half = HEAD_DIM // 2
    inv = ROPE_BASE ** (-jnp.arange(half, dtype=F32) / half)
    ang = jnp.arange(l).astype(F32)[:, None] * inv[None]
    cs, sn = jnp.cos(ang), jnp.sin(ang)
    return jnp.concatenate([cs, cs], -1), jnp.concatenate([-sn, sn], -1)


def _diff_tables(l):
    n = DIFF_HD // 4
    pos = jnp.arange(l)
    inv = ROPE_BASE ** (-jnp.arange(n, dtype=F32) / n)
    ang = jnp.concatenate([(pos // GRID_W).astype(F32)[:, None] * inv[None],
                           (pos % GRID_W).astype(F32)[:, None] * inv[None]], -1)
    cs, sn, zero = jnp.cos(ang), jnp.sin(ang), jnp.zeros_like(ang)
    c64 = jnp.concatenate([cs, cs], -1)
    s1 = jnp.concatenate([-sn, zero], -1)
    s2 = jnp.concatenate([zero, sn], -1)
    return tuple(jnp.concatenate([t, t], -1) for t in (c64, s1, s2))


def kernel(x, c, ctx, c_ctx, mod_w, mod_b, ln_g, ln_b, e_w_in, e_conv, e_a_log, e_dt_bias, e_norm_w, e_ret_decay,
           e_w_out, o_w_qkv, o_lambda, o_subln_w, o_w_out, f_w_gate, f_w_up, f_conv, f_w_down):
    depth = mod_w.shape[0]
    b, l, d = x.shape
    lc = ctx.shape[1]
    alpha = (2 * depth) ** 0.25
    n_rows = -(-(b + 1) // 8) * 8
    cc = jnp.concatenate([c, c_ctx[None, :], jnp.zeros((n_rows - b - 1, d), F32)], axis=0)
    mod = _mod_call(cc, mod_w, mod_b).reshape(depth, n_rows, 1, 6 * d)[:, :b + 1]
    ret_cos, ret_sin = _ret_tables(l)
    diff_tabs = _diff_tables(l)
    for li in range(depth):
        last = li == depth - 1
        i = li // 2
        mod_l = mod[li]
        if li % 2 == 0:
            w_main, w_gate, conv_h, hp, w_o = _even_weights(e_w_in[i], e_conv[i], e_a_log[i], e_dt_bias[i],
                                                            e_ret_decay[i], e_w_out[i])
            pc, gc = _proj_call(ctx, mod_l, 0, True, [w_main, w_gate], [BF16, F32], "even_proj_ctx")
            pt, gt = _proj_call(x, mod_l, 0, False, [w_main, w_gate], [BF16, F32], "even_proj_lat")
            yc, yt = _even_call(pc, pt, gc, gt, conv_h, hp, e_norm_w[i], ret_cos, ret_sin)
        else:
            lam_init = 0.8 - 0.6 * math.exp(-0.3 * li)
            w_qkv = o_w_qkv[i].astype(BF16)
            (pc,) = _proj_call(ctx, mod_l, 0, True, [w_qkv], [BF16], "odd_proj_ctx")
            (pt,) = _proj_call(x, mod_l, 0, False, [w_qkv], [BF16], "odd_proj_lat")
            yt = _attn_call(pt, pc, diff_tabs, o_lambda[i], o_subln_w[i], lam_init)
            w_o = o_w_out[i].astype(BF16)
            yc = None
            if not last:
                raise NotImplementedError("context update after a differential-attention layer")
        wgu, wd = _gate_up_weights(f_w_gate[li], f_w_up[li]), f_w_down[li].astype(BF16)
        wcv = f_conv[li].reshape(9, -1)
        x1, hf = _outln_call(yt, x, mod_l, 2, False, w_o, ln_g[li, 0], ln_b[li, 0], alpha, 3, "mix_out_lat")
        x = _ffn_call(hf, wgu, wcv, wd, GRID_W, True, x1, mod_l, 5, False, ln_g[li, 1], ln_b[li, 1], alpha, "ffn_lat")
        if not last:
            c1, hcf = _outln_call(yc, ctx, mod_l, 2, True, w_o, ln_g[li, 0], ln_b[li, 0], alpha, 3, "mix_out_ctx")
            grp = max(g for g in range(1, b + 1) if b % g == 0 and g * lc <= l)
            ctx = _ffn_call(hcf.reshape(b // grp, grp * lc, d), wgu, wcv, wd, lc, False,
                            c1.reshape(b // grp, grp * lc, d), mod_l, 5, True, ln_g[li, 1], ln_b[li, 1], alpha,
                            "ffn_ctx").reshape(b, lc, d)
    return x
```

```python
import functools
import math

import jax
import jax.numpy as jnp
from jax import lax
from jax.experimental import pallas as pl
from jax.experimental.pallas import tpu as pltpu

F32 = jnp.float32
BF16 = jnp.bfloat16

GRID_W = 64
DN_HEADS = 4
RET_HEADS = 4
HEAD_DIM = 128
DIFF_HEADS = 8
DIFF_HD = 64
DIFF_DV = 2 * DIFF_HD
ROPE_BASE = 10000.0
LN_EPS = 1e-5
NORM_EPS = 1e-6

CHUNK = 128
INV_BASE = 16
PAR_CHUNKS = 4
PROJ_ROWS = 1024
ATTN_TQ = 2048
ATTN_SUB = 128
SCORE_AHEAD = 3
LOG2_E = 1.4426950408889634
GATE_AHEAD = 2
FFN_ROWS = 512
FFN_LEAD_EDGES = (128, 256)
FFN_TF = 256
FFN_SUB = 64
BF16_ROWS = 16
VMEM_LIMIT = 56 * 1024 * 1024


def _dot(a, b):
    return jnp.dot(a, b, preferred_element_type=F32)


def _silu(x):
    return x * jax.nn.sigmoid(x)


def _softplus(x):
    return jnp.maximum(x, 0.0) + jnp.log1p(jnp.exp(-jnp.abs(x)))


def _cparams(sem):
    return pltpu.CompilerParams(dimension_semantics=sem, vmem_limit_bytes=VMEM_LIMIT)


def _mod_kernel(cc_ref, w_ref, b_ref, o_ref):
    a = _silu(cc_ref[...])
    w = w_ref[0]
    a_hi = a.astype(BF16)
    a_lo = (a - a_hi.astype(F32)).astype(BF16)
    w_hi = w.astype(BF16)
    w_lo = (w - w_hi.astype(F32)).astype(BF16)
    o_ref[0] = _dot(a_hi, w_hi) + _dot(a_hi, w_lo) + _dot(a_lo, w_hi) + b_ref[0]


def _mod_call(cc, mod_w, mod_b):
    depth, d, n = mod_w.shape
    r = cc.shape[0]
    tn = min(n, 1536)
    return pl.pallas_call(
        _mod_kernel,
        grid=(depth, n // tn),
        in_specs=[pl.BlockSpec((r, d), lambda l, j: (0, 0)),
                  pl.BlockSpec((1, d, tn), lambda l, j: (l, 0, j)),
                  pl.BlockSpec((1, 1, tn), lambda l, j: (l, 0, j))],
        out_specs=pl.BlockSpec((1, r, tn), lambda l, j: (l, 0, j)),
        out_shape=jax.ShapeDtypeStruct((depth, r, n), F32),
        compiler_params=_cparams(("parallel", "parallel")),
        name="adaln_maps",
    )(cc, mod_w, mod_b.reshape(depth, 1, n))


def _mod_spec(d, k, is_ctx, nb, ngrid):
    if ngrid == 2:
        imap = (lambda b, i: (nb, 0, k)) if is_ctx else (lambda b, i: (b, 0, k))
    else:
        imap = (lambda b, i, j: (nb, 0, k)) if is_ctx else (lambda b, i, j: (b, 0, k))
    return pl.BlockSpec((1, 1, d), imap)


def _proj_kernel(x_ref, sh_ref, sc_ref, *refs, n_w, tn):
    w_refs, o_refs = refs[:n_w], refs[n_w:]
    h = (x_ref[0] * (1.0 + sc_ref[0]) + sh_ref[0]).astype(BF16)
    for w_ref, o_ref in zip(w_refs, o_refs):
        n = w_ref.shape[1]
        for n0 in range(0, n, tn):
            n1 = min(n0 + tn, n)
            o_ref[0, :, n0:n1] = _dot(h, w_ref[:, n0:n1]).astype(o_ref.dtype)


def _proj_call(x, mod_l, k_shift, is_ctx, weights, out_dtypes, name):
    b, l, d = x.shape
    nb = mod_l.shape[0] - 1
    tm = min(l, PROJ_ROWS)
    in_specs = [pl.BlockSpec((1, tm, d), lambda bi, i: (bi, i, 0)),
                _mod_spec(d, k_shift, is_ctx, nb, 2),
                _mod_spec(d, k_shift + 1, is_ctx, nb, 2)]
    out_specs, out_shape = [], []
    for w, dt in zip(weights, out_dtypes):
        n = w.shape[1]
        in_specs.append(pl.BlockSpec((d, n), lambda bi, i: (0, 0)))
        out_specs.append(pl.BlockSpec((1, tm, n), lambda bi, i: (bi, i, 0)))
        out_shape.append(jax.ShapeDtypeStruct((b, l, n), dt))
    return pl.pallas_call(
        functools.partial(_proj_kernel, n_w=len(weights), tn=512),
        grid=(b, l // tm),
        in_specs=in_specs, out_specs=out_specs, out_shape=out_shape,
        compiler_params=_cparams(("parallel", "parallel")),
        name=name,
    )(x, mod_l, mod_l, *weights)


def _outln_kernel(*refs, alpha, has_w, has_h):
    refs = list(refs)
    y_ref, x_ref, g_ref = refs[:3]
    refs = refs[3:]
    w_ref = refs.pop(0) if has_w else None
    lng_ref, lnb_ref = refs[:2]
    refs = refs[2:]
    if has_h:
        sh_ref, sc_ref, o_ref, h_ref = refs
    else:
        (o_ref,) = refs
    t = _dot(y_ref[0], w_ref[...]) if has_w else y_ref[0].astype(F32)
    z = alpha * x_ref[0] + g_ref[0] * t
    mu = jnp.mean(z, axis=-1, keepdims=True)
    zc = z - mu
    var = jnp.mean(zc * zc, axis=-1, keepdims=True)
    o = zc * lax.rsqrt(var + LN_EPS) * lng_ref[...] + lnb_ref[...]
    o_ref[0] = o
    if has_h:
        h_ref[0] = (o * (1.0 + sc_ref[0]) + sh_ref[0]).astype(BF16)


def _outln_call(y, x, mod_l, k_gate, is_ctx, w, ln_g, ln_b, alpha, k_shift_next, name):
    b, l, d = x.shape
    kdim = y.shape[-1]
    nb = mod_l.shape[0] - 1
    tm = min(l, PROJ_ROWS)
    has_w, has_h = w is not None, k_shift_next is not None
    args = [y, x, mod_l]
    in_specs = [pl.BlockSpec((1, tm, kdim), lambda bi, i: (bi, i, 0)),
                pl.BlockSpec((1, tm, d), lambda bi, i: (bi, i, 0)),
                _mod_spec(d, k_gate, is_ctx, nb, 2)]
    if has_w:
        args.append(w)
        in_specs.append(pl.BlockSpec((kdim, d), lambda bi, i: (0, 0)))
    args += [ln_g.reshape(1, d), ln_b.reshape(1, d)]
    in_specs += [pl.BlockSpec((1, d), lambda bi, i: (0, 0))] * 2
    out_specs = [pl.BlockSpec((1, tm, d), lambda bi, i: (bi, i, 0))]
    out_shape = [jax.ShapeDtypeStruct((b, l, d), F32)]
    if has_h:
        args += [mod_l, mod_l]
        in_specs += [_mod_spec(d, k_shift_next, is_ctx, nb, 2), _mod_spec(d, k_shift_next + 1, is_ctx, nb, 2)]
        out_specs.append(pl.BlockSpec((1, tm, d), lambda bi, i: (bi, i, 0)))
        out_shape.append(jax.ShapeDtypeStruct((b, l, d), BF16))
    res = pl.pallas_call(
        functools.partial(_outln_kernel, alpha=alpha, has_w=has_w, has_h=has_h),
        grid=(b, l // tm),
        in_specs=in_specs, out_specs=out_specs, out_shape=out_shape,
        compiler_params=_cparams(("parallel", "parallel")),
        name=name,
    )(*args)
    return res if has_h else res[0]


def _ffn_kernel(h_ref, wgu_ref, wc_ref, wd_ref, x_ref, g_ref, lng_ref, lnb_ref, o_ref, apad, ubuf,
                *, l, w, vertical, pad, rb, alpha):
    f = pl.program_id(1)
    tf = wd_ref.shape[0]

    @pl.when(f == 0)
    def _():
        o_ref[0] = jnp.zeros(o_ref.shape[1:], F32)
        apad[0:pad, :] = jnp.zeros((pad, tf), F32)
        apad[pad + l:pad + l + pad, :] = jnp.zeros((pad, tf), F32)

    edges = [e for e in FFN_LEAD_EDGES if e < rb and e < l] + list(range(rb, l, rb)) + [l] if l > rb else [l]
    blocks = list(zip([0] + edges[:-1], edges))
    nblk = len(blocks)
    dis = (-1, 0, 1) if vertical else (0,)
    wrow = [wc_ref[k:k + 1, :] for k in range(9)]
    row8 = lax.broadcasted_iota(jnp.int32, (8, tf), 0)

    def gate(i):
        lo, hi = blocks[i]
        gu = _dot(h_ref[0, lo:hi, :], wgu_ref[...])
        apad[pad + lo:pad + hi, :] = gu[:, 0:tf]
        ubuf[lo:hi, :] = gu[:, tf:2 * tf]

    def colsum(dj, lo, n):
        tot = None
        for di in dis:
            term = apad[lo + di * w:lo + di * w + n, :] * wrow[(di + 1) * 3 + dj + 1]
            tot = term if tot is None else tot + term
        return tot

    def down(i, t):
        lo, hi = blocks[i]
        o_ref[0, lo:hi, :] += _dot(t, wd_ref[...])

    for i in range(min(GATE_AHEAD, nblk)):
        gate(i)
    t_prev = None
    for i in range(nblk):
        if i + GATE_AHEAD < nblk:
            gate(i + GATE_AHEAD)
        r0, r1 = blocks[i]
        base = pad + r0
        if t_prev is not None:
            down(i - 1, t_prev)
        ts = []
        for s0 in range(0, r1 - r0, FFN_SUB):
            lo = base + s0
            n_ext = FFN_SUB + 16
            left = pltpu.roll(colsum(-1, lo - 8, n_ext), 1, 0)[8:8 + FFN_SUB, :]
            right = pltpu.roll(colsum(1, lo - 8, n_ext), n_ext - 1, 0)[8:8 + FFN_SUB, :]
            if (r0 + s0) % w == 0:
                left = jnp.concatenate([jnp.where(row8 == 0, 0.0, left[0:8, :]), left[8:, :]], axis=0)
            if (r0 + s0 + FFN_SUB) % w == 0:
                right = jnp.concatenate([right[:FFN_SUB - 8, :], jnp.where(row8 == 7, 0.0, right[FFN_SUB - 8:, :])],
                                        axis=0)
            conv = colsum(0, lo, FFN_SUB) + left + right
            ts.append((_silu(conv) * ubuf[r0 + s0:r0 + s0 + FFN_SUB, :]).astype(BF16))
        t_prev = jnp.concatenate(ts, axis=0)
    down(nblk - 1, t_prev)

    @pl.when(f == pl.num_programs(1) - 1)
    def _():
        for lo in range(0, l, FFN_SUB):
            z = alpha * x_ref[0, lo:lo + FFN_SUB, :] + g_ref[0] * o_ref[0, lo:lo + FFN_SUB, :]
            mu = jnp.mean(z, axis=-1, keepdims=True)
            zc = z - mu
            var = jnp.mean(zc * zc, axis=-1, keepdims=True)
            o_ref[0, lo:lo + FFN_SUB, :] = zc * lax.rsqrt(var + LN_EPS) * lng_ref[...] + lnb_ref[...]


def _ffn_call(hf, w_gate_up, w_conv, w_down, grid_w, vertical, x, mod_l, k_gate, is_ctx, ln_g, ln_b, alpha, name):
    b, l, d = hf.shape
    dff = w_down.shape[0]
    tf = FFN_TF
    rb = min(l, FFN_ROWS)
    pad = grid_w + 8 if vertical else 8
    assert grid_w % FFN_SUB == 0 and l % rb == 0 and rb % FFN_SUB == 0 and all(e % FFN_SUB == 0 for e in FFN_LEAD_EDGES)
    return pl.pallas_call(
        functools.partial(_ffn_kernel, l=l, w=grid_w, vertical=vertical, pad=pad, rb=rb, alpha=alpha),
        grid=(b, dff // tf),
        in_specs=[pl.BlockSpec((1, l, d), lambda bi, f: (bi, 0, 0)),
                  pl.BlockSpec((d, 2 * tf), lambda bi, f: (0, f)),
                  pl.BlockSpec((9, tf), lambda bi, f: (0, f)),
                  pl.BlockSpec((tf, d), lambda bi, f: (f, 0)),
                  pl.BlockSpec((1, l, d), lambda bi, f: (bi, 0, 0)),
                  _mod_spec(d, k_gate, is_ctx, mod_l.shape[0] - 1, 2),
                  pl.BlockSpec((1, d), lambda bi, f: (0, 0)),
                  pl.BlockSpec((1, d), lambda bi, f: (0, 0))],
        out_specs=pl.BlockSpec((1, l, d), lambda bi, f: (bi, 0, 0)),
        out_shape=jax.ShapeDtypeStruct((b, l, d), F32),
        scratch_shapes=[pltpu.VMEM((l + 2 * pad, tf), F32), pltpu.VMEM((l, tf), F32)],
        compiler_params=_cparams(("parallel", "arbitrary")),
        name=name,
    )(hf, w_gate_up, w_conv, w_down, x, mod_l, ln_g.reshape(1, d), ln_b.reshape(1, d))


def _even_kernel(pc_ref, pt_ref, gc_ref, gt_ref, cw_ref, hp_ref, nw_ref, cos_ref, sin_ref,
                 yc_ref, yt_ref,
                 u_s, wq_s, qkk_s, cd_s, ri_s, kvf_s, kvb_s, rqf_s, rqb_s, of_s, ob_s, rb_s, *, nc, nt):
    c = CHUNK
    hd = HEAD_DIM
    ntot = nc + nt
    ri = lax.broadcasted_iota(jnp.int32, (c, c), 0)
    ci = lax.broadcasted_iota(jnp.int32, (c, c), 1)
    rif = ri.astype(F32)
    cif = ci.astype(F32)
    eye = (ri == ci).astype(F32)

    def same_block(size):
        sh = size.bit_length() - 1
        return jnp.right_shift(ri, sh) == jnp.right_shift(ci, sh)

    diag_mask = same_block(INV_BASE).astype(F32)
    off_masks = []
    size = INV_BASE
    while size < c:
        off_masks.append(jnp.where(same_block(2 * size), 1.0, 0.0) - jnp.where(same_block(size), 1.0, 0.0))
        size *= 2
    hp = hp_ref[0]
    a_log8, dt8 = hp[0:8, :], hp[8:16, :]
    lg_f = -jnp.exp(hp[16:17, :])
    lg_b = -jnp.exp(hp[17:18, :])
    dec_bi = (jnp.where(ri >= ci, jnp.exp(lg_f * (rif - cif)), 0.0)
              + jnp.where(ci >= ri, jnp.exp(lg_b * (cif - rif)), 0.0))
    gq_f = jnp.exp(lg_f * (rif + 1.0))
    gq_b = jnp.exp(lg_b * (c - rif))
    gk_f = jnp.exp(lg_f * (c - 1.0 - cif))
    gk_b = jnp.exp(lg_b * cif)
    cd_f = jnp.exp(lg_f * c)
    cd_b = jnp.exp(lg_b * c)
    row8 = lax.broadcasted_iota(jnp.int32, (8, c), 0)
    lane8 = lax.broadcasted_iota(jnp.int32, (8, c), 1)

    def gate_rows(g):
        gt8 = g.T[0:8, :]
        la = -jnp.exp(a_log8) * _softplus(gt8 + dt8)
        pre, suf = la, la
        s = 1
        while s < c:
            pre = pre + jnp.where(lane8 >= s, pltpu.roll(pre, s, 1), 0.0)
            suf = suf + jnp.where(lane8 < c - s, pltpu.roll(suf, c - s, 1), 0.0)
            s *= 2
        cum = jnp.where(row8 == 0, pre, suf)
        ecum = pltpu.roll(jnp.exp(cum), 4, 0)
        return jnp.where(row8 < 2, cum, jnp.where(row8 < 4, jax.nn.sigmoid(gt8), jnp.where(row8 < 6, ecum, 0.0)))

    def chunk_prep(src_ref, gsrc_ref, m, ns, off, rot):
        ln = ns * c
        r0 = pl.multiple_of(m * c, c)
        rows = pl.ds(r0, c)
        x = src_ref[0, rows, 0:3 * hd].astype(F32)
        pstart = pl.multiple_of(jnp.maximum(r0 - BF16_ROWS, 0), BF16_ROWS)
        nstart = pl.multiple_of(jnp.minimum(r0 + c, ln - BF16_ROWS), BF16_ROWS)
        prev = src_ref[0, pl.ds(pstart, BF16_ROWS), 0:3 * hd].astype(F32)[BF16_ROWS - 1:BF16_ROWS, :]
        nxt = src_ref[0, pl.ds(nstart, BF16_ROWS), 0:3 * hd].astype(F32)[0:1, :]
        prev = prev * jnp.where(m > 0, 1.0, 0.0)
        nxt = nxt * jnp.where(m < ns - 1, 1.0, 0.0)
        rr = lax.broadcasted_iota(jnp.int32, (c, 3 * hd), 0)
        xm1 = jnp.where(rr == 0, prev, pltpu.roll(x, 1, 0))
        xp1 = jnp.where(rr == c - 1, nxt, pltpu.roll(x, c - 1, 0))
        cw = cw_ref[0]
        y = _silu(xm1 * cw[0:1, :] + x * cw[1:2, :] + xp1 * cw[2:3, :])
        q = y[:, 0:hd]
        k = y[:, hd:2 * hd]
        q = q * (lax.rsqrt(jnp.sum(q * q, axis=-1, keepdims=True) + NORM_EPS) * (hd ** -0.5))
        k = k * lax.rsqrt(jnp.sum(k * k, axis=-1, keepdims=True) + NORM_EPS)
        xt8 = gate_rows(gsrc_ref[0, rows, :])
        xg = jnp.concatenate([xt8, jnp.zeros((c - 8, c), F32)], axis=0).T
        rq = src_ref[0, rows, 4 * hd:5 * hd].astype(F32)
        rk = src_ref[0, rows, 5 * hd:6 * hd].astype(F32)
        if rot:
            cs, sn = cos_ref[rows, :], sin_ref[rows, :]
            rq = rq * cs + pltpu.roll(rq, hd // 2, 1) * sn
            rk = rk * cs + pltpu.roll(rk, hd // 2, 1) * sn
        return dict(n=m + off, q=q, k=k, kt=k.T, v=y[:, 2 * hd:3 * hd], xt8=xt8, xg=xg,
                    rq=rq, rkt=(rk * (hd ** -0.5)).T, rv=src_ref[0, rows, 6 * hd:7 * hd])

    def chunks_work(preps):
        for pr in preps:
            pr["kt16"] = pr["kt"].astype(BF16)
        kks = [_dot(pr["k"].astype(BF16), pr["kt16"]) for pr in preps]
        qks = [_dot(pr["q"].astype(BF16), pr["kt16"]) for pr in preps]
        scs = [_dot(pr["rq"].astype(BF16), pr["rkt"].astype(BF16)) for pr in preps]
        chains = []
        for pr, kk, qk in zip(preps, kks, qks):
            for d in range(2):
                xg, xt8 = pr["xg"], pr["xt8"]
                g, beta, eg = xg[:, d:d + 1], xg[:, 2 + d:3 + d], xg[:, 4 + d:5 + d]
                g_row = xt8[d:d + 1, :]
                last = 0 if d else c - 1
                g_last = xt8[d:d + 1, last:last + 1]
                e = jnp.exp(jnp.minimum(g - g_row, 0.0))
                incl = (ri <= ci) if d else (ri >= ci)
                strict = (ri < ci) if d else (ri > ci)
                nm = -(beta * kk) * jnp.where(strict, e, 0.0)
                nd = nm * diag_mask
                chains.append(dict(pr=pr, d=d, nm=nm, nd16=nd.astype(BF16), p=eye + nd, beta=beta, eg=eg,
                                   qkm=qk * jnp.where(incl, e, 0.0), kgt=pr["kt"] * jnp.exp(g_last - g_row),
                                   cd=jnp.broadcast_to(jnp.exp(g_last), (8, c))))
        for ch in chains:
            ch["npow"] = _dot(ch["nd16"], ch["nd16"])
        lvl = 2
        while lvl * 2 < INV_BASE:
            rs = [_dot(jnp.concatenate([ch["p"], ch["npow"]], axis=0).astype(BF16), ch["npow"].astype(BF16))
                  for ch in chains]
            for ch, r in zip(chains, rs):
                ch["p"] = ch["p"] + r[0:c, :]
                ch["npow"] = r[c:, :]
            lvl *= 2
        rs = [_dot(ch["p"].astype(BF16), ch["npow"].astype(BF16)) for ch in chains]
        for ch, r in zip(chains, rs):
            ch["p"] = ch["p"] + r
        for om in off_masks:
            xs = [_dot(ch["p"].astype(BF16), (ch["nm"] * om).astype(BF16)) for ch in chains]
            rs = [_dot(x.astype(BF16), ch["p"].astype(BF16)) for ch, x in zip(chains, xs)]
            for ch, r in zip(chains, rs):
                ch["p"] = ch["p"] + r
        sols = [_dot(ch["p"].astype(BF16),
                     jnp.concatenate([ch["pr"]["v"] * ch["beta"], ch["pr"]["k"] * (ch["beta"] * ch["eg"])],
                                     axis=1).astype(BF16)) for ch in chains]
        rets = [_dot(jnp.concatenate([sc * dec_bi, pr["rkt"] * gk_f, pr["rkt"] * gk_b], axis=0).astype(BF16), pr["rv"])
                for pr, sc in zip(preps, scs)]
        for ch, sol in zip(chains, sols):
            idx = ch["d"] * ntot + ch["pr"]["n"]
            u_s[idx] = sol[:, 0:hd]
            wq_s[idx] = jnp.concatenate([sol[:, hd:2 * hd], ch["pr"]["q"] * ch["eg"]], axis=0).astype(BF16)
            qkk_s[idx] = jnp.concatenate([ch["qkm"], ch["kgt"]], axis=0).astype(BF16)
            cd_s[idx] = ch["cd"]
        for pr, r in zip(preps, rets):
            n = pr["n"]
            ri_s[n] = r[0:c, :]
            kvf_s[n] = r[c:2 * c, :]
            kvb_s[n] = r[2 * c:3 * c, :]
            rqf_s[n] = (pr["rq"] * gq_f).astype(BF16)
            rqb_s[n] = (pr["rq"] * gq_b).astype(BF16)

    def segment(src_ref, gsrc_ref, ns, off, rot):
        par = math.gcd(ns, PAR_CHUNKS)

        def body(j, carry):
            chunks_work([chunk_prep(src_ref, gsrc_ref, j * par + t, ns, off, rot) for t in range(par)])
            return carry

        lax.fori_loop(0, ns // par, body, 0)

    segment(pc_ref, gc_ref, nc, 0, False)
    segment(pt_ref, gt_ref, nt, nc, True)

    def emit(src_ref, dst_ref, m, o, r):
        rows = pl.ds(pl.multiple_of(m * c, c), c)
        zg = src_ref[0, rows, 3 * hd:4 * hd].astype(F32)
        dn = o * lax.rsqrt(jnp.mean(o * o, axis=-1, keepdims=True) + NORM_EPS) * nw_ref[...] * _silu(zg)
        mu = jnp.mean(r, axis=-1, keepdims=True)
        rc = r - mu
        rg = src_ref[0, rows, 7 * hd:8 * hd].astype(F32)
        rt = rc * lax.rsqrt(jnp.mean(rc * rc, axis=-1, keepdims=True) + NORM_EPS) * _silu(rg)
        dst_ref[0, rows, 0:hd] = dn.astype(dst_ref.dtype)
        dst_ref[0, rows, hd:2 * hd] = rt.astype(dst_ref.dtype)

    def step(i, carry, tail):
        s_f, s_b, t_f, t_b = carry
        nf = i
        nbk = jnp.where(i < nc, nc - 1 - i, ntot - 1 - (i - nc))
        idxs = (nf, ntot + nbk)
        r1s = [_dot(wq_s[idx], st.astype(BF16)) for idx, st in zip(idxs, (s_f, s_b))]
        r_f = ri_s[nf] + _dot(rqf_s[nf], t_f.astype(BF16))
        r_b = _dot(rqb_s[nbk], t_b.astype(BF16))
        r2s = [_dot(qkk_s[idx], (u_s[idx] - r1[0:c, :]).astype(BF16)) for idx, r1 in zip(idxs, r1s)]
        o_f, o_b = [r1[c:, :] + r2[0:c, :] for r1, r2 in zip(r1s, r2s)]
        s_f, s_b = [cd_s[idx][0:1, :] * st + r2[c:, :] for idx, st, r2 in zip(idxs, (s_f, s_b), r2s)]
        t_f = cd_f * t_f + kvf_s[nf]
        t_b = cd_b * t_b + kvb_s[nbk]
        if tail:
            other_b, other_rb, other_f, other_rf = ob_s[nf], rb_s[nf], of_s[nbk], ri_s[nbk]
            emit(pt_ref, yt_ref, nf - nc, o_f + other_b, r_f + other_rb)
            emit(pt_ref, yt_ref, nbk - nc, other_f + o_b, other_rf + r_b)
        else:
            of_s[nf] = o_f
            ob_s[nbk] = o_b
            ri_s[nf] = r_f
            rb_s[nbk] = r_b
        return s_f, s_b, t_f, t_b

    n_head = nc + nt // 2
    z = jnp.zeros((hd, hd), F32)
    carry = lax.fori_loop(0, n_head, functools.partial(step, tail=False), (z, z, z, z))
    lax.fori_loop(n_head, ntot, functools.partial(step, tail=True), carry)

    def ctx_finish(m, carry):
        emit(pc_ref, yc_ref, m, of_s[m] + ob_s[m], ri_s[m] + rb_s[m])
        return carry

    lax.fori_loop(0, nc, ctx_finish, 0)


def _even_call(pc, pt, gc, gt, conv_h, hp, norm_w, cos_t, sin_t):
    b, lc, _ = pc.shape
    lt = pt.shape[1]
    nc, nt = lc // CHUNK, lt // CHUNK
    assert lc % CHUNK == 0 and lt % (2 * CHUNK) == 0
    ntot = nc + nt
    nh = DN_HEADS
    hd = HEAD_DIM
    blk = 8 * hd
    f32_scr = pltpu.VMEM((ntot, CHUNK, CHUNK), F32)
    bf16_scr = pltpu.VMEM((ntot, CHUNK, CHUNK), BF16)
    return pl.pallas_call(
        functools.partial(_even_kernel, nc=nc, nt=nt),
        grid=(b, nh),
        in_specs=[pl.BlockSpec((1, lc, blk), lambda bi, h: (bi, 0, h)),
                  pl.BlockSpec((1, lt, blk), lambda bi, h: (bi, 0, h)),
                  pl.BlockSpec((1, lc, hd), lambda bi, h: (bi, 0, h)),
                  pl.BlockSpec((1, lt, hd), lambda bi, h: (bi, 0, h)),
                  pl.BlockSpec((1, 3, 3 * hd), lambda bi, h: (h, 0, 0)),
                  pl.BlockSpec((1, 24, hd), lambda bi, h: (h, 0, 0)),
                  pl.BlockSpec((1, hd), lambda bi, h: (0, 0)),
                  pl.BlockSpec((lt, hd), lambda bi, h: (0, 0)),
                  pl.BlockSpec((lt, hd), lambda bi, h: (0, 0))],
        out_specs=[pl.BlockSpec((1, lc, 2 * hd), lambda bi, h: (bi, 0, h)),
                   pl.BlockSpec((1, lt, 2 * hd), lambda bi, h: (bi, 0, h))],
        out_shape=[jax.ShapeDtypeStruct((b, lc, nh * 2 * hd), BF16),
                   jax.ShapeDtypeStruct((b, lt, nh * 2 * hd), BF16)],
        scratch_shapes=[pltpu.VMEM((2 * ntot, CHUNK, CHUNK), F32),
                        pltpu.VMEM((2 * ntot, 2 * CHUNK, CHUNK), BF16),
                        pltpu.VMEM((2 * ntot, 2 * CHUNK, CHUNK), BF16),
                        pltpu.VMEM((2 * ntot, 8, CHUNK), F32),
                        f32_scr, f32_scr, f32_scr,
                        bf16_scr, bf16_scr,
                        f32_scr, f32_scr, f32_scr],
        compiler_params=_cparams(("parallel", "arbitrary")),
        name="deltanet_retention",
    )(pc, pt, gc, gt, conv_h, hp, norm_w.reshape(1, hd), cos_t, sin_t)


def _rot2d(x, c_ref, s1_ref, s2_ref):
    return x * c_ref[...] + pltpu.roll(x, 96, 1) * s1_ref[...] + pltpu.roll(x, 32, 1) * s2_ref[...]


def _attn_kernel(q_ref, kt_ref, vt_ref, kc_ref, vc_ref, qc_ref, qs1_ref, qs2_ref, kc_t, ks1_t, ks2_t,
                 lam_ref, sw_ref, o_ref, kfull, vfull, *, lt, lc, lam_init):
    qi = pl.program_id(2)
    hd = DIFF_HD
    lk = lt + lc

    @pl.when(qi == 0)
    def _():
        for r0 in range(0, lt, 128):
            kr = _rot2d(kt_ref[0, r0:r0 + 128, :].astype(F32), kc_t.at[r0:r0 + 128, :],
                        ks1_t.at[r0:r0 + 128, :], ks2_t.at[r0:r0 + 128, :]).T
            kfull[0, :, r0:r0 + 128] = kr[0:hd, :].astype(BF16)
            kfull[1, :, r0:r0 + 128] = kr[hd:2 * hd, :].astype(BF16)
        for r0 in range(0, lc, 128):
            kr = kc_ref[0, r0:r0 + 128, :].astype(F32).T
            kfull[0, :, lt + r0:lt + r0 + 128] = kr[0:hd, :].astype(BF16)
            kfull[1, :, lt + r0:lt + r0 + 128] = kr[hd:2 * hd, :].astype(BF16)
        lane = lax.broadcasted_iota(jnp.int32, (lk, 128), 1)
        vfull[0:lt, 0:128] = vt_ref[0].astype(vfull.dtype)
        vfull[lt:lk, 0:128] = vc_ref[0].astype(vfull.dtype)
        vfull[:, 128:256] = jnp.where(lane == 0, 1.0, 0.0).astype(BF16)

    lp = lam_ref[...]
    lam = (jnp.exp(jnp.sum(lp[0:1, :] * lp[1:2, :], axis=-1, keepdims=True))
           - jnp.exp(jnp.sum(lp[2:3, :] * lp[3:4, :], axis=-1, keepdims=True)) + lam_init)
    q = (_rot2d(q_ref[0].astype(F32), qc_ref, qs1_ref, qs2_ref) * (hd ** -0.5 * LOG2_E)).astype(BF16)
    tq = q.shape[0]
    units = [(r0, comp) for r0 in range(0, tq, ATTN_SUB) for comp in range(2)]

    def score(r0, comp):
        return _dot(q[r0:r0 + ATTN_SUB, comp * hd:(comp + 1) * hd], kfull[comp])

    def weighted(s):
        p = jnp.exp2(s - jnp.max(s, axis=-1, keepdims=True)).astype(BF16)
        r = _dot(p, vfull[...])
        return r[:, 0:128] / r[:, 128:129]

    scores = [score(*u) for u in units[:SCORE_AHEAD]]
    outs = []
    for i in range(len(units)):
        if i + SCORE_AHEAD < len(units):
            scores.append(score(*units[i + SCORE_AHEAD]))
        outs.append(weighted(scores[i]))
        scores[i] = None
    for j, r0 in enumerate(range(0, tq, ATTN_SUB)):
        o = outs[2 * j] - lam * outs[2 * j + 1]
        o = o * lax.rsqrt(jnp.mean(o * o, axis=-1, keepdims=True) + NORM_EPS) * sw_ref[...] * (1.0 - lam_init)
        o_ref[0, r0:r0 + ATTN_SUB, :] = o.astype(o_ref.dtype)


def _attn_call(pt, pc, tabs, lam_p, subln_w, lam_init):
    b, lt, _ = pt.shape
    lc = pc.shape[1]
    nh = DIFF_HEADS
    tq = min(lt, ATTN_TQ)
    ct, s1t, s2t = tabs
    tab_q = pl.BlockSpec((tq, 128), lambda bi, h, qi: (qi, 0))
    tab_k = pl.BlockSpec((lt, 128), lambda bi, h, qi: (0, 0))
    return pl.pallas_call(
        functools.partial(_attn_kernel, lt=lt, lc=lc, lam_init=lam_init),
        grid=(b, nh, lt // tq),
        in_specs=[pl.BlockSpec((1, tq, 128), lambda bi, h, qi: (bi, qi, h)),
                  pl.BlockSpec((1, lt, 128), lambda bi, h, qi: (bi, 0, nh + h)),
                  pl.BlockSpec((1, lt, 128), lambda bi, h, qi: (bi, 0, 2 * nh + h)),
                  pl.BlockSpec((1, lc, 128), lambda bi, h, qi: (bi, 0, nh + h)),
                  pl.BlockSpec((1, lc, 128), lambda bi, h, qi: (bi, 0, 2 * nh + h)),
                  tab_q, tab_q, tab_q, tab_k, tab_k, tab_k,
                  pl.BlockSpec((4, DIFF_HD), lambda bi, h, qi: (0, 0)),
                  pl.BlockSpec((1, 128), lambda bi, h, qi: (0, 0))],
        out_specs=pl.BlockSpec((1, tq, 128), lambda bi, h, qi: (bi, qi, h)),
        out_shape=jax.ShapeDtypeStruct((b, lt, nh * DIFF_DV), BF16),
        scratch_shapes=[pltpu.VMEM((2, DIFF_HD, lt + lc), BF16), pltpu.VMEM((lt + lc, 256), BF16)],
        compiler_params=_cparams(("parallel", "parallel", "arbitrary")),
        name="diff_attention",
    )(pt, pt, pt, pc, pc, ct, s1t, s2t, ct, s1t, s2t, lam_p, subln_w.reshape(1, 128))


def _even_weights(w_in, conv_w, a_log, dt_bias, ret_decay, w_out):
    nh, hd = DN_HEADS, HEAD_DIM
    d = w_in.shape[0]
    o_z, o_a, o_b = 3 * nh * hd, 4 * nh * hd, 4 * nh * hd + 2 * nh
    o_rq = o_b + 2 * nh
    parts = jnp.concatenate([w_in[:, 0:o_z].reshape(d, 3, nh, hd), w_in[:, o_z:o_a].reshape(d, 1, nh, hd),
                             w_in[:, o_rq:].reshape(d, 4, nh, hd)], axis=1)
    w_main = parts.transpose(0, 2, 1, 3).reshape(d, nh * 8 * hd).astype(BF16)
    gates = jnp.stack([w_in[:, o_a:o_b].reshape(d, 2, nh), w_in[:, o_b:o_rq].reshape(d, 2, nh)], axis=1)
    gates = gates.reshape(d, 4, nh).transpose(0, 2, 1)
    w_gate = jnp.pad(gates, ((0, 0), (0, 0), (0, hd - 4))).reshape(d, nh * hd).astype(BF16)
    conv_h = conv_w.reshape(3, 3, nh, hd).transpose(2, 0, 1, 3).reshape(nh, 3, 3 * hd).astype(F32)
    zeros = jnp.zeros((nh, 6), F32)
    rows = jnp.concatenate([a_log.T, zeros, dt_bias.T, zeros, ret_decay.T, zeros], axis=1)
    hp = jnp.broadcast_to(rows[:, :, None], (nh, 24, hd)).astype(F32)
    w_o = w_out.reshape(2, nh, hd, -1).transpose(1, 0, 2, 3).reshape(2 * nh * hd, -1).astype(BF16)
    return w_main, w_gate, conv_h, hp, w_o


def _gate_up_weights(w_gate, w_up):
    d, dff = w_gate.shape
    nf = dff // FFN_TF
    both = jnp.concatenate([w_gate.astype(BF16).reshape(d, nf, FFN_TF), w_up.astype(BF16).reshape(d, nf, FFN_TF)],
                           axis=2)
    return both.reshape(d, nf * 2 * FFN_TF)


def _ret_tables(l):
    half = HEAD_DIM // 2
    inv = ROPE_BASE ** (-jnp.arange(half, dtype=F32) / half)
    ang = jnp.arange(l).astype(F32)[:, None] * inv[None]
    cs, sn = jnp.cos(ang), jnp.sin(ang)
    return jnp.concatenate([cs, cs], -1), jnp.concatenate([-sn, sn], -1)


def _diff_tables(l):
    n = DIFF_HD // 4
    pos = jnp.arange(l)
    inv = ROPE_BASE ** (-jnp.arange(n, dtype=F32) / n)
    ang = jnp.concatenate([(pos // GRID_W).astype(F32)[:, None] * inv[None],
                           (pos % GRID_W).astype(F32)[:, None] * inv[None]], -1)
    cs, sn, zero = jnp.cos(ang), jnp.sin(ang), jnp.zeros_like(ang)
    c64 = jnp.concatenate([cs, cs], -1)
    s1 = jnp.concatenate([-sn, zero], -1)
    s2 = jnp.concatenate([zero, sn], -1)
    return tuple(jnp.concatenate([t, t], -1) for t in (c64, s1, s2))


def kernel(x, c, ctx, c_ctx, mod_w, mod_b, ln_g, ln_b, e_w_in, e_conv, e_a_log, e_dt_bias, e_norm_w, e_ret_decay,
           e_w_out, o_w_qkv, o_lambda, o_subln_w, o_w_out, f_w_gate, f_w_up, f_conv, f_w_down):
    depth = mod_w.shape[0]
    b, l, d = x.shape
    lc = ctx.shape[1]
    alpha = (2 * depth) ** 0.25
    n_rows = -(-(b + 1) // 8) * 8
    cc = jnp.concatenate([c, c_ctx[None, :], jnp.zeros((n_rows - b - 1, d), F32)], axis=0)
    mod = _mod_call(cc, mod_w, mod_b).reshape(depth, n_rows, 1, 6 * d)[:, :b + 1]
    ret_cos, ret_sin = _ret_tables(l)
    diff_tabs = _diff_tables(l)
    for li in range(depth):
        last = li == depth - 1
        i = li // 2
        mod_l = mod[li]
        if li % 2 == 0:
            w_main, w_gate, conv_h, hp, w_o = _even_weights(e_w_in[i], e_conv[i], e_a_log[i], e_dt_bias[i],
                                                            e_ret_decay[i], e_w_out[i])
            pc, gc = _proj_call(ctx, mod_l, 0, True, [w_main, w_gate], [BF16, F32], "even_proj_ctx")
            pt, gt = _proj_call(x, mod_l, 0, False, [w_main, w_gate], [BF16, F32], "even_proj_lat")
            yc, yt = _even_call(pc, pt, gc, gt, conv_h, hp, e_norm_w[i], ret_cos, ret_sin)
        else:
            lam_init = 0.8 - 0.6 * math.exp(-0.3 * li)
            w_qkv = o_w_qkv[i].astype(BF16)
            (pc,) = _proj_call(ctx, mod_l, 0, True, [w_qkv], [BF16], "odd_proj_ctx")
            (pt,) = _proj_call(x, mod_l, 0, False, [w_qkv], [BF16], "odd_proj_lat")
            yt = _attn_call(pt, pc, diff_tabs, o_lambda[i], o_subln_w[i], lam_init)
            w_o = o_w_out[i].astype(BF16)
            yc = None
            if not last:
                raise NotImplementedError("context update after a differential-attention layer")
        wgu, wd = _gate_up_weights(f_w_gate[li], f_w_up[li]), f_w_down[li].astype(BF16)
        wcv = f_conv[li].reshape(9, -1)
        x1, hf = _outln_call(yt, x, mod_l, 2, False, w_o, ln_g[li, 0], ln_b[li, 0], alpha, 3, "mix_out_lat")
        x = _ffn_call(hf, wgu, wcv, wd, GRID_W, True, x1, mod_l, 5, False, ln_g[li, 1], ln_b[li, 1], alpha, "ffn_lat")
        if not last:
            c1, hcf = _outln_call(yc, ctx, mod_l, 2, True, w_o, ln_g[li, 0], ln_b[li, 0], alpha, 3, "mix_out_ctx")
            grp = max(g for g in range(1, b + 1) if b % g == 0 and g * lc <= l)
            ctx = _ffn_call(hcf.reshape(b // grp, grp * lc, d), wgu, wcv, wd, lc, False,
                            c1.reshape(b // grp, grp * lc, d), mod_l, 5, True, ln_g[li, 1], ln_b[li, 1], alpha,
                            "ffn_ctx").reshape(b, lc, d)
    return x
```
